```python
import math, functools
import jax, jax.numpy as jnp
from jax import lax
import numpy as np

D_MODEL = 4096
BATCH = 4
SEQ = 2048
DEPTH = 4
DEC_BATCH = 128
DEC_SEQ = 8
PAST_LEN = 8192
PAGE_SIZE = 128

N_EVEN = (DEPTH + 1) // 2
N_ODD = DEPTH // 2

S5_WIDTH = D_MODEL // 2
S5_GROUP = 16
S5_GROUPS = S5_WIDTH // S5_GROUP
S5_STATE = 64
DT_MIN = 1e-3
DT_MAX = 1e-1

MLA_HEADS = 16
MLA_NOPE = 128
MLA_ROPE = 64
MLA_V = 128
MLA_KV_RANK = 512
MLA_Q_RANK = D_MODEL // 4
MLA_WIDTH = MLA_HEADS * MLA_V
MLA_SCALE = (MLA_NOPE + MLA_ROPE) ** -0.5
EVEN_IN = S5_WIDTH + MLA_Q_RANK + MLA_KV_RANK + MLA_ROPE

C_HEAD_DIM = 128
C_PATTERNS = ((128, 1), (512, 4), (2048, 16))
C_GROUPS = len(C_PATTERNS)
C_SLOTS = 8
C_KV_HEADS = 4
C_REP = C_SLOTS // C_KV_HEADS
C_WINDOW_MAX = max(w for w, _ in C_PATTERNS)
C_Q_WIDTH = C_GROUPS * C_SLOTS * C_HEAD_DIM
C_KV_WIDTH = C_KV_HEADS * C_HEAD_DIM
ODD_IN = C_Q_WIDTH + 2 * C_KV_WIDTH
C_OUT_WIDTH = C_SLOTS * C_HEAD_DIM
C_SCALE = C_HEAD_DIM ** -0.5

MEM_LEN = 256
MEM_HEADS = 4
MEM_HEAD_DIM = 128
MEM_WIDTH = MEM_HEADS * MEM_HEAD_DIM
MEM_SCALE = MEM_HEAD_DIM ** -0.5

D_FF = 11008
N_EXPERTS = 8
TOP_K = 2
D_FF_EXPERT = D_MODEL

ROPE_THETA = 10000.0
LN_EPS = 1e-5
RMS_EPS = 1e-6
Q_BLOCK = 128
DEEPNORM_ALPHA = (2.0 * DEPTH) ** 0.25
DEEPNORM_BETA = (8.0 * DEPTH) ** -0.25
POOL_NUM = 5
POOL_DEN = 4

kernel_name = 'hybrid_s5_mla_dilated_decoder_step'


def layer_norm(x, g, b):
    xf = x.astype(jnp.float32)
    mu = jnp.mean(xf, -1, keepdims=True)
    var = jnp.mean(jnp.square(xf - mu), -1, keepdims=True)
    return ((xf - mu) * lax.rsqrt(var + LN_EPS) * g.astype(jnp.float32) + b.astype(jnp.float32)).astype(x.dtype)


def rms_norm(x, g):
    xf = x.astype(jnp.float32)
    return (xf * lax.rsqrt(jnp.mean(xf * xf, -1, keepdims=True) + RMS_EPS) * g.astype(jnp.float32)).astype(x.dtype)


def rope(x, pos):
    half = x.shape[-1] // 2
    inv = ROPE_THETA ** (-jnp.arange(half, dtype=jnp.float32) / half)
    ang = pos.astype(jnp.float32)[:, None] * inv[None, :]
    ang = ang.reshape((1, pos.shape[0]) + (1,) * (x.ndim - 3) + (half,))
    cos, sin = jnp.cos(ang), jnp.sin(ang)
    xf = x.astype(jnp.float32)
    x1, x2 = xf[..., :half], xf[..., half:]
    return jnp.concatenate([x1 * cos - x2 * sin, x2 * cos + x1 * sin], -1).astype(x.dtype)


def _ssm_combine(e1, e2):
    a1r, a1i, b1r, b1i = e1
    a2r, a2i, b2r, b2i = e2
    return (a2r * a1r - a2i * a1i, a2r * a1i + a2i * a1r,
            a2r * b1r - a2i * b1i + b2r, a2r * b1i + a2i * b1r + b2i)


def s5_mixer(u, h0_re, h0_im, lam_re, lam_im, log_dt, b_re, b_im, c_re, c_im, d_skip, w_glu):
    f32 = jnp.float32
    bsz, t = u.shape[:2]
    uf = u.astype(f32).reshape(bsz, t, S5_GROUPS, S5_GROUP)
    dt = jnp.exp(log_dt.astype(f32))[:, None]
    lr, li = lam_re.astype(f32), lam_im.astype(f32)
    decay = jnp.exp(lr * dt)
    a_re, a_im = decay * jnp.cos(li * dt), decay * jnp.sin(li * dt)
    inv_den = 1.0 / (lr * lr + li * li)
    f_re = ((a_re - 1.0) * lr + a_im * li) * inv_den
    f_im = (a_im * lr - (a_re - 1.0) * li) * inv_den
    br, bi = b_re.astype(f32), b_im.astype(f32)
    bb_re = f_re[..., None] * br - f_im[..., None] * bi
    bb_im = f_re[..., None] * bi + f_im[..., None] * br
    bu_re = jnp.einsum('btgc,gpc->btgp', uf, bb_re)
    bu_im = jnp.einsum('btgc,gpc->btgp', uf, bb_im)
    ar = jnp.broadcast_to(a_re, bu_re.shape)
    ai = jnp.broadcast_to(a_im, bu_re.shape)
    pr, pim, sr, si = lax.associative_scan(_ssm_combine, (ar, ai, bu_re, bu_im), axis=1)
    h0r, h0i = h0_re.astype(f32)[:, None], h0_im.astype(f32)[:, None]
    h_re = sr + pr * h0r - pim * h0i
    h_im = si + pr * h0i + pim * h0r
    y = (jnp.einsum('btgp,gcp->btgc', h_re, c_re.astype(f32))
         - jnp.einsum('btgp,gcp->btgc', h_im, c_im.astype(f32))
         + d_skip.astype(f32) * uf)
    y = jax.nn.gelu(y.reshape(bsz, t, S5_WIDTH)).astype(u.dtype)
    out = y * jax.nn.sigmoid(y @ w_glu)
    return out, h_re[:, -1], h_im[:, -1]


def mla_attend_prompt(q_lat, q_pe, c_kv, k_pe):
    bsz, t = q_lat.shape[:2]
    nblk = t // Q_BLOCK
    kpos = jnp.arange(t)

    def block(args):
        i, ql, qp = args
        qpos = i * Q_BLOCK + jnp.arange(Q_BLOCK)
        s = (jnp.einsum('bqhc,bkc->bhqk', ql, c_kv)
             + jnp.einsum('bqhr,bkr->bhqk', qp, k_pe)).astype(jnp.float32) * MLA_SCALE
        s = jnp.where(kpos[None, :] <= qpos[:, None], s, -jnp.inf)
        p = jax.nn.softmax(s, -1).astype(c_kv.dtype)
        return jnp.einsum('bhqk,bkc->bqhc', p, c_kv)

    ql = q_lat.reshape(bsz, nblk, Q_BLOCK, MLA_HEADS, MLA_KV_RANK).swapaxes(0, 1)
    qp = q_pe.reshape(bsz, nblk, Q_BLOCK, MLA_HEADS, MLA_ROPE).swapaxes(0, 1)
    o = lax.map(block, (jnp.arange(nblk), ql, qp))
    return o.swapaxes(0, 1).reshape(bsz, t, MLA_HEADS, MLA_KV_RANK)


def mla_attend_sample(q_lat, q_pe, c_kv, k_pe, lat_past, pe_past):
    t = q_lat.shape[1]
    past = lat_past.shape[1]
    s_past = (jnp.einsum('bqhc,bkc->bhqk', q_lat, lat_past)
              + jnp.einsum('bqhr,bkr->bhqk', q_pe, pe_past)).astype(jnp.float32) * MLA_SCALE
    s_new = (jnp.einsum('bqhc,bkc->bhqk', q_lat, c_kv)
             + jnp.einsum('bqhr,bkr->bhqk', q_pe, k_pe)).astype(jnp.float32) * MLA_SCALE
    s_new = jnp.where(jnp.tril(jnp.ones((t, t), bool)), s_new, -jnp.inf)
    p = jax.nn.softmax(jnp.concatenate([s_past, s_new], -1), -1).astype(c_kv.dtype)
    return (jnp.einsum('bhqk,bkc->bqhc', p[..., :past], lat_past)
            + jnp.einsum('bhqk,bkc->bqhc', p[..., past:], c_kv))


def even_mixer(x, pos, h0_re, h0_im, attend, w_in, lam_re, lam_im, log_dt, b_re, b_im,
               c_re, c_im, d_skip, w_glu, q_norm, kv_norm, w_uq, w_uk, w_uv, w_out):
    bsz, t = x.shape[:2]
    h = x @ w_in
    o1 = S5_WIDTH
    o2 = o1 + MLA_Q_RANK
    o3 = o2 + MLA_KV_RANK
    u, cq, ckv, kr = h[..., :o1], h[..., o1:o2], h[..., o2:o3], h[..., o3:]
    s5_out, h_re, h_im = s5_mixer(u, h0_re, h0_im, lam_re, lam_im, log_dt, b_re, b_im,
                                  c_re, c_im, d_skip, w_glu)
    q = jnp.einsum('btr,rhe->bthe', rms_norm(cq, q_norm), w_uq)
    q_pe = rope(q[..., MLA_NOPE:], pos)
    q_lat = jnp.einsum('bthn,chn->bthc', q[..., :MLA_NOPE], w_uk)
    c_kv = rms_norm(ckv, kv_norm)
    k_pe = rope(kr, pos)
    o_lat = attend(q_lat, q_pe, c_kv, k_pe)
    mla_out = jnp.einsum('bthc,chv->bthv', o_lat, w_uv).reshape(bsz, t, MLA_WIDTH)
    mix = jnp.concatenate([s5_out, mla_out], -1) @ w_out
    return mix, h_re, h_im, c_kv, k_pe


def odd_project(x, pos, w_in):
    bsz, t = x.shape[:2]
    h = x @ w_in
    q = h[..., :C_Q_WIDTH].reshape(bsz, t, C_GROUPS, C_SLOTS, C_HEAD_DIM)
    k = h[..., C_Q_WIDTH:C_Q_WIDTH + C_KV_WIDTH].reshape(bsz, t, C_KV_HEADS, C_HEAD_DIM)
    v = h[..., C_Q_WIDTH + C_KV_WIDTH:].reshape(bsz, t, C_KV_HEADS, C_HEAD_DIM)
    return rope(q, pos), rope(k, pos), v


def dilated_attend(q, k, v, qpos):
    f32 = jnp.float32
    bsz, tq = q.shape[:2]
    outs, lses = [], []
    for g, (window, dil) in enumerate(C_PATTERNS):
        n_k = window // dil + 1
        idx = qpos[:, None] - dil * jnp.arange(n_k)[None, :]
        valid = idx >= 0
        idx_c = jnp.maximum(idx, 0)
        kg = jnp.take(k, idx_c, axis=1)
        vg = jnp.take(v, idx_c, axis=1)
        qg = q[:, :, g].reshape(bsz, tq, C_KV_HEADS, C_REP, C_HEAD_DIM)
        s = jnp.einsum('bqgrd,bqkgd->bqgrk', qg, kg).astype(f32) * C_SCALE
        s = jnp.where(valid[None, :, None, None, :], s, -jnp.inf)
        m = jnp.max(s, -1, keepdims=True)
        e = jnp.exp(s - m)
        den = jnp.sum(e, -1, keepdims=True)
        outs.append(jnp.einsum('bqgrk,bqkgd->bqgrd', e, vg.astype(f32)) / den)
        lses.append(m + jnp.log(den))
    alpha = jax.nn.softmax(jnp.stack(lses, 0), axis=0)
    out = jnp.sum(alpha * jnp.stack(outs, 0), 0)
    return out.reshape(bsz, tq, C_OUT_WIDTH).astype(q.dtype)


def odd_mixer_prompt(x, pos, w_in, w_out):
    bsz, t = x.shape[:2]
    q, k, v = odd_project(x, pos, w_in)
    nblk = t // Q_BLOCK
    qb = q.reshape(bsz, nblk, Q_BLOCK, C_GROUPS, C_SLOTS, C_HEAD_DIM).swapaxes(0, 1)
    pb = jnp.arange(t).reshape(nblk, Q_BLOCK)
    o = lax.map(lambda a: dilated_attend(a[0], k, v, a[1]), (qb, pb))
    o = o.swapaxes(0, 1).reshape(bsz, t, C_OUT_WIDTH)
    keep = min(C_WINDOW_MAX, t)
    return o @ w_out, k[:, t - keep:], v[:, t - keep:]


def odd_mixer_sample(x, pos, buf_k, buf_v, w_in, w_out):
    t = x.shape[1]
    q, k_new, v_new = odd_project(x, pos, w_in)
    k_all = jnp.concatenate([buf_k, k_new], 1)
    v_all = jnp.concatenate([buf_v, v_new], 1)
    qpos = buf_k.shape[1] + jnp.arange(t)
    o = dilated_attend(q, k_all, v_all, qpos)
    return o @ w_out, k_all[:, t:], v_all[:, t:]


def memory_kv(mem, w_k, w_v):
    bsz, m = mem.shape[:2]
    return ((mem @ w_k).reshape(bsz, m, MEM_HEADS, MEM_HEAD_DIM),
            (mem @ w_v).reshape(bsz, m, MEM_HEADS, MEM_HEAD_DIM))


def cross_attend(x, mk, mv, w_q, w_o):
    bsz, t = x.shape[:2]
    q = (x @ w_q).reshape(bsz, t, MEM_HEADS, MEM_HEAD_DIM)
    s = jnp.einsum('bthd,bmhd->bhtm', q, mk).astype(jnp.float32) * MEM_SCALE
    p = jax.nn.softmax(s, -1).astype(x.dtype)
    o = jnp.einsum('bhtm,bmhd->bthd', p, mv).reshape(bsz, t, MEM_WIDTH)
    return o @ w_o


def swiglu(x, w_gate, w_up, w_down):
    return (jax.nn.silu(x @ w_gate) * (x @ w_up)) @ w_down


def moe_ffn(x, w_router, w_gate, w_up, w_down):
    logits = (x @ w_router).astype(jnp.float32)
    top_v, top_i = lax.top_k(logits, TOP_K)
    gates = jax.nn.softmax(top_v, -1)
    combine = jnp.sum(jax.nn.one_hot(top_i, N_EXPERTS, dtype=jnp.float32) * gates[..., None], -2)
    y = jnp.zeros(x.shape, jnp.float32)
    for e in range(N_EXPERTS):
        y = y + combine[..., e:e + 1] * swiglu(x, w_gate[e], w_up[e], w_down[e]).astype(jnp.float32)
    return y.astype(x.dtype)


def setup_inputs(seed: int = 0) -> dict:
    key = jax.random.key(seed)
    keys = jax.random.split(key, 48)
    counter = [0]
    f32 = jnp.float32

    def nxt():
        k = keys[counter[0]]
        counter[0] += 1
        return k

    def nrm(shape, scale=1.0):
        return jax.random.normal(nxt(), shape, f32) * scale

    n_pages = PAST_LEN // PAGE_SIZE
    n_phys = (DEC_BATCH * n_pages * POOL_NUM) // POOL_DEN
    c_buf = min(C_WINDOW_MAX, PAST_LEN)
    beta = DEEPNORM_BETA
    page_table = jax.random.permutation(nxt(), n_phys)[:DEC_BATCH * n_pages]
    page_table = page_table.reshape(DEC_BATCH, n_pages).astype(jnp.int32)
    lam_im0 = jnp.pi * jnp.arange(S5_STATE, dtype=f32)
    return {
        'x_prompt': nrm((BATCH, SEQ, D_MODEL)),
        'x_sample': nrm((DEC_BATCH, DEC_SEQ, D_MODEL)),
        'mem_prompt': nrm((BATCH, MEM_LEN, D_MODEL)),
        'cache_mla_lat': nrm((N_EVEN, n_phys, PAGE_SIZE, MLA_KV_RANK)),
        'cache_mla_pe': nrm((N_EVEN, n_phys, PAGE_SIZE, MLA_ROPE)),
        'cache_c_k': nrm((N_ODD, DEC_BATCH, c_buf, C_KV_HEADS, C_HEAD_DIM)),
        'cache_c_v': nrm((N_ODD, DEC_BATCH, c_buf, C_KV_HEADS, C_HEAD_DIM)),
        'cache_mem_k': nrm((DEPTH, DEC_BATCH, MEM_LEN, MEM_HEADS, MEM_HEAD_DIM)),
        'cache_mem_v': nrm((DEPTH, DEC_BATCH, MEM_LEN, MEM_HEADS, MEM_HEAD_DIM)),
        'state_s5_re': nrm((N_EVEN, DEC_BATCH, S5_GROUPS, S5_STATE), 0.5),
        'state_s5_im': nrm((N_EVEN, DEC_BATCH, S5_GROUPS, S5_STATE), 0.5),
        'page_table': page_table,
        'ln_g': 1.0 + nrm((DEPTH, 3, D_MODEL), 0.01),
        'ln_b': nrm((DEPTH, 3, D_MODEL), 0.01),
        'w_in_even': nrm((N_EVEN, D_MODEL, EVEN_IN), D_MODEL ** -0.5),
        's5_lambda_re': -0.5 + nrm((N_EVEN, S5_GROUPS, S5_STATE), 0.01),
        's5_lambda_im': lam_im0 + nrm((N_EVEN, S5_GROUPS, S5_STATE), 0.01),
        's5_log_dt': jax.random.uniform(nxt(), (N_EVEN, S5_GROUPS), f32,
                                        minval=math.log(DT_MIN), maxval=math.log(DT_MAX)),
        's5_b_re': nrm((N_EVEN, S5_GROUPS, S5_STATE, S5_GROUP), (2.0 * S5_GROUP) ** -0.5),
        's5_b_im': nrm((N_EVEN, S5_GROUPS, S5_STATE, S5_GROUP), (2.0 * S5_GROUP) ** -0.5),
        's5_c_re': nrm((N_EVEN, S5_GROUPS, S5_GROUP, S5_STATE), (2.0 * S5_STATE) ** -0.5),
        's5_c_im': nrm((N_EVEN, S5_GROUPS, S5_GROUP, S5_STATE), (2.0 * S5_STATE) ** -0.5),
        's5_d': nrm((N_EVEN, S5_GROUPS, S5_GROUP), 0.5),
        's5_w_glu': nrm((N_EVEN, S5_WIDTH, S5_WIDTH), S5_WIDTH ** -0.5),
        'mla_q_norm': 1.0 + nrm((N_EVEN, MLA_Q_RANK), 0.01),
        'mla_kv_norm': 1.0 + nrm((N_EVEN, MLA_KV_RANK), 0.01),
        'mla_w_uq': nrm((N_EVEN, MLA_Q_RANK, MLA_HEADS, MLA_NOPE + MLA_ROPE), MLA_Q_RANK ** -0.5),
        'mla_w_uk': nrm((N_EVEN, MLA_KV_RANK, MLA_HEADS, MLA_NOPE), MLA_KV_RANK ** -0.5),
        'mla_w_uv': nrm((N_EVEN, MLA_KV_RANK, MLA_HEADS, MLA_V), MLA_KV_RANK ** -0.5),
        'w_out_even': nrm((N_EVEN, S5_WIDTH + MLA_WIDTH, D_MODEL), beta * (S5_WIDTH + MLA_WIDTH) ** -0.5),
        'w_in_odd': nrm((N_ODD, D_MODEL, ODD_IN), D_MODEL ** -0.5),
        'w_out_odd': nrm((N_ODD, C_OUT_WIDTH, D_MODEL), beta * C_OUT_WIDTH ** -0.5),
        'mem_w_q': nrm((DEPTH, D_MODEL, MEM_WIDTH), D_MODEL ** -0.5),
        'mem_w_k': nrm((DEPTH, D_MODEL, MEM_WIDTH), D_MODEL ** -0.5),
        'mem_w_v': nrm((DEPTH, D_MODEL, MEM_WIDTH), D_MODEL ** -0.5),
        'mem_w_o': nrm((DEPTH, MEM_WIDTH, D_MODEL), beta * MEM_WIDTH ** -0.5),
        'ffn_w_gate': nrm((N_EVEN, D_MODEL, D_FF), D_MODEL ** -0.5),
        'ffn_w_up': nrm((N_EVEN, D_MODEL, D_FF), D_MODEL ** -0.5),
        'ffn_w_down': nrm((N_EVEN, D_FF, D_MODEL), beta * D_FF ** -0.5),
        'moe_w_router': nrm((N_ODD, D_MODEL, N_EXPERTS), D_MODEL ** -0.5),
        'moe_w_gate': nrm((N_ODD, N_EXPERTS, D_MODEL, D_FF_EXPERT), D_MODEL ** -0.5),
        'moe_w_up': nrm((N_ODD, N_EXPERTS, D_MODEL, D_FF_EXPERT), D_MODEL ** -0.5),
        'moe_w_down': nrm((N_ODD, N_EXPERTS, D_FF_EXPERT, D_MODEL), beta * D_FF_EXPERT ** -0.5),
    }


def reference(x_prompt, x_sample, mem_prompt, cache_mla_lat, cache_mla_pe, cache_c_k, cache_c_v,
              cache_mem_k, cache_mem_v, state_s5_re, state_s5_im, page_table,
              ln_g, ln_b, w_in_even, s5_lambda_re, s5_lambda_im, s5_log_dt, s5_b_re, s5_b_im,
              s5_c_re, s5_c_im, s5_d, s5_w_glu, mla_q_norm, mla_kv_norm, mla_w_uq, mla_w_uk,
              mla_w_uv, w_out_even, w_in_odd, w_out_odd, mem_w_q, mem_w_k, mem_w_v, mem_w_o,
              ffn_w_gate, ffn_w_up, ffn_w_down, moe_w_router, moe_w_gate, moe_w_up, moe_w_down):
    bp, tp = x_prompt.shape[:2]
    bs, ts = x_sample.shape[:2]
    n_pages = page_table.shape[1]
    past_len = n_pages * PAGE_SIZE
    pos_p = jnp.arange(tp, dtype=jnp.int32)
    pos_s = past_len + jnp.arange(ts, dtype=jnp.int32)
    a = DEEPNORM_ALPHA
    xp, xs = x_prompt, x_sample
    lat_p, pe_p, lat_s, pe_s = [], [], [], []
    s5r_p, s5i_p, s5r_s, s5i_s = [], [], [], []
    ck_p, cv_p, ck_s, cv_s = [], [], [], []
    mk_p, mv_p = [], []
    for l in range(DEPTH):
        if l % 2 == 0:
            e = l // 2
            w_e = (w_in_even[e], s5_lambda_re[e], s5_lambda_im[e], s5_log_dt[e], s5_b_re[e], s5_b_im[e],
                   s5_c_re[e], s5_c_im[e], s5_d[e], s5_w_glu[e], mla_q_norm[e], mla_kv_norm[e],
                   mla_w_uq[e], mla_w_uk[e], mla_w_uv[e], w_out_even[e])
            h0 = jnp.zeros((bp, S5_GROUPS, S5_STATE), jnp.float32)
            mix_p, hr, hi, ckv, kpe = even_mixer(xp, pos_p, h0, h0, mla_attend_prompt, *w_e)
            s5r_p.append(hr)
            s5i_p.append(hi)
            lat_p.append(ckv)
            pe_p.append(kpe)
            lat_past = cache_mla_lat[e][page_table].reshape(bs, past_len, MLA_KV_RANK)
            pe_past = cache_mla_pe[e][page_table].reshape(bs, past_len, MLA_ROPE)
            attend_s = functools.partial(mla_attend_sample, lat_past=lat_past, pe_past=pe_past)
            mix_s, hr, hi, ckv, kpe = even_mixer(xs, pos_s, state_s5_re[e], state_s5_im[e], attend_s, *w_e)
            s5r_s.append(hr)
            s5i_s.append(hi)
            lat_s.append(ckv)
            pe_s.append(kpe)
        else:
            o = l // 2
            mix_p, kb, vb = odd_mixer_prompt(xp, pos_p, w_in_odd[o], w_out_odd[o])
            ck_p.append(kb)
            cv_p.append(vb)
            mix_s, kb, vb = odd_mixer_sample(xs, pos_s, cache_c_k[o], cache_c_v[o], w_in_odd[o], w_out_odd[o])
            ck_s.append(kb)
            cv_s.append(vb)
        xp = layer_norm(a * xp + mix_p, ln_g[l, 0], ln_b[l, 0])
        xs = layer_norm(a * xs + mix_s, ln_g[l, 0], ln_b[l, 0])
        mk, mv = memory_kv(mem_prompt, mem_w_k[l], mem_w_v[l])
        mk_p.append(mk)
        mv_p.append(mv)
        xp = layer_norm(a * xp + cross_attend(xp, mk, mv, mem_w_q[l], mem_w_o[l]), ln_g[l, 1], ln_b[l, 1])
        xs = layer_norm(a * xs + cross_attend(xs, cache_mem_k[l], cache_mem_v[l], mem_w_q[l], mem_w_o[l]),
                        ln_g[l, 1], ln_b[l, 1])
        if l % 2 == 0:
            e = l // 2
            f_p = swiglu(xp, ffn_w_gate[e], ffn_w_up[e], ffn_w_down[e])
            f_s = swiglu(xs, ffn_w_gate[e], ffn_w_up[e], ffn_w_down[e])
        else:
            o = l // 2
            f_p = moe_ffn(xp, moe_w_router[o], moe_w_gate[o], moe_w_up[o], moe_w_down[o])
            f_s = moe_ffn(xs, moe_w_router[o], moe_w_gate[o], moe_w_up[o], moe_w_down[o])
        xp = layer_norm(a * xp + f_p, ln_g[l, 2], ln_b[l, 2])
        xs = layer_norm(a * xs + f_s, ln_g[l, 2], ln_b[l, 2])
    return (xp, xs,
            jnp.stack(lat_p), jnp.stack(pe_p), jnp.stack(lat_s), jnp.stack(pe_s),
            jnp.stack(s5r_p), jnp.stack(s5i_p), jnp.stack(s5r_s), jnp.stack(s5i_s),
            jnp.stack(ck_p), jnp.stack(cv_p), jnp.stack(ck_s), jnp.stack(cv_s),
            jnp.stack(mk_p), jnp.stack(mv_p))
```

```python
import functools
import math

import jax
import jax.numpy as jnp
from jax import lax
from jax.experimental import pallas as pl
from jax.experimental.pallas import tpu as pltpu

F32 = jnp.float32
BF16 = jnp.bfloat16
I32 = jnp.int32

VMEM_LIMIT_BYTES = 56 * 1024 * 1024
LANES = 128
SUBLANES = 8

D_MODEL = 4096
DEPTH = 4
PAGE_SIZE = 128
S5_WIDTH = D_MODEL // 2
S5_GROUP = 16
S5_GROUPS = S5_WIDTH // S5_GROUP
S5_STATE = 64
S5_FLAT = S5_GROUPS * S5_STATE
S5_SUPER = 16
S5_NSUPER = S5_GROUPS // S5_SUPER
MLA_HEADS = 16
MLA_NOPE = 128
MLA_ROPE = 64
MLA_V = 128
MLA_KV_RANK = 512
MLA_Q_RANK = D_MODEL // 4
MLA_WIDTH = MLA_HEADS * MLA_V
MLA_SCALE = (MLA_NOPE + MLA_ROPE) ** -0.5
C_HEAD_DIM = 128
C_PATTERNS = ((128, 1), (512, 4), (2048, 16))
C_GROUPS = len(C_PATTERNS)
C_SLOTS = 8
C_KV_HEADS = 4
C_REP = C_SLOTS // C_KV_HEADS
C_SPAN = 128
C_Q_WIDTH = C_GROUPS * C_SLOTS * C_HEAD_DIM
C_KV_WIDTH = C_KV_HEADS * C_HEAD_DIM
C_OUT_WIDTH = C_SLOTS * C_HEAD_DIM
C_SCALE = C_HEAD_DIM ** -0.5
MEM_HEADS = 4
MEM_HEAD_DIM = 128
MEM_WIDTH = MEM_HEADS * MEM_HEAD_DIM
MEM_SCALE = MEM_HEAD_DIM ** -0.5
N_EXPERTS = 8
TOP_K = 2
ROPE_THETA = 10000.0
LN_EPS = 1e-5
RMS_EPS = 1e-6
DEEPNORM_ALPHA = (2.0 * DEPTH) ** 0.25
NEG_INF = float("-inf")

assert all(w // d == C_SPAN for w, d in C_PATTERNS)


def _params(*sem):
    return pltpu.CompilerParams(dimension_semantics=sem, vmem_limit_bytes=VMEM_LIMIT_BYTES)


def _dot(a, b):
    return jnp.dot(a, b, preferred_element_type=F32)


def _dot_nt(a, b):
    return lax.dot_general(a, b, (((1,), (1,)), ((), ())), preferred_element_type=F32)


def _gmm_kernel(tg_ref, x_ref, *refs, n_w, epilogue, n_extra, row_chunk):
    w_refs = refs[:n_w]
    extra_refs = refs[n_w:n_w + n_extra]
    o_ref = refs[n_w + n_extra]
    wb_refs = refs[n_w + n_extra + 1:]
    i = pl.program_id(1)
    prev = tg_ref[jnp.maximum(i - 1, 0)]
    changed = jnp.logical_or(i == 0, tg_ref[i] != prev)

    @pl.when(changed)
    def _():
        depth = w_refs[0].shape[0]
        step = 512 if depth % 512 == 0 else depth
        for w_ref, wb_ref in zip(w_refs, wb_refs):
            for r in range(0, depth, step):
                wb_ref[r:r + step, :] = w_ref[r:r + step, :].astype(BF16)

    tm = x_ref.shape[0]
    for r in range(0, tm, row_chunk):
        x = x_ref[r:r + row_chunk, :].astype(BF16)
        acc = _dot(x, wb_refs[0][...])
        if epilogue == "swiglu":
            up = _dot(x, wb_refs[1][...])
            acc = acc * jax.nn.sigmoid(acc) * up
        elif epilogue == "glu":
            acc = extra_refs[0][r:r + row_chunk, :] * jax.nn.sigmoid(acc)
        o_ref[r:r + row_chunk, :] = acc.astype(o_ref.dtype)


def gmm(x, ws, tile_group=None, *, tm, tn, out_dtype, epilogue="none", extras=(), x_col_block=0,
        row_block_offset=0, n_rows=None, name="gmm"):
    ws = tuple(w if w.ndim == 3 else w[None] for w in ws)
    depth, n_out = ws[0].shape[1], ws[0].shape[2]
    rows = x.shape[0] if n_rows is None else n_rows
    assert rows % tm == 0
    ni, nj = rows // tm, pl.cdiv(n_out, tn)
    if tile_group is None:
        tile_group = jnp.zeros((ni,), I32)
    row_chunk = min(tm, 256)
    kern = functools.partial(_gmm_kernel, n_w=len(ws), epilogue=epilogue, n_extra=len(extras),
                             row_chunk=row_chunk)
    in_specs = [pl.BlockSpec((tm, depth), lambda j, i, tg: (i + row_block_offset, x_col_block))]
    in_specs += [pl.BlockSpec((None, depth, tn), lambda j, i, tg: (tg[i], 0, j)) for _ in ws]
    in_specs += [pl.BlockSpec((tm, tn), lambda j, i, tg: (i + row_block_offset, j)) for _ in extras]
    return pl.pallas_call(
        kern,
        out_shape=jax.ShapeDtypeStruct((rows, n_out), out_dtype),
        grid_spec=pltpu.PrefetchScalarGridSpec(
            num_scalar_prefetch=1, grid=(nj, ni), in_specs=in_specs,
            out_specs=pl.BlockSpec((tm, tn), lambda j, i, tg: (i, j)),
            scratch_shapes=[pltpu.VMEM((depth, tn), BF16) for _ in ws]),
        compiler_params=_params("arbitrary", "arbitrary"),
        name=name,
    )(tile_group, x, *ws, *extras)


def _mm_kernel(x_ref, w_ref, o_ref, acc_ref, *, depth, tk):
    k = pl.program_id(2)
    nk = pl.num_programs(2)

    @pl.when(k == 0)
    def _():
        acc_ref[...] = jnp.zeros_like(acc_ref)

    rem = depth % tk

    def full():
        acc_ref[...] += _dot(x_ref[...].astype(BF16), w_ref[...].astype(BF16))

    if rem == 0:
        full()
    else:
        pl.when(k < nk - 1)(full)

        @pl.when(k == nk - 1)
        def _():
            acc_ref[...] += _dot(x_ref[:, :rem].astype(BF16), w_ref[:rem, :].astype(BF16))

    @pl.when(k == nk - 1)
    def _():
        o_ref[...] = acc_ref[...].astype(o_ref.dtype)


def mm(x, w, *, tm, tn, tk, out_dtype, name="mm"):
    rows, depth = x.shape
    n_out = w.shape[1]
    assert rows % tm == 0 and n_out % tn == 0 and (depth % tk) % LANES == 0
    grid = (rows // tm, n_out // tn, pl.cdiv(depth, tk))
    return pl.pallas_call(
        functools.partial(_mm_kernel, depth=depth, tk=tk),
        out_shape=jax.ShapeDtypeStruct((rows, n_out), out_dtype),
        grid=grid,
        in_specs=[pl.BlockSpec((tm, tk), lambda i, j, k: (i, k)),
                  pl.BlockSpec((tk, tn), lambda i, j, k: (k, j))],
        out_specs=pl.BlockSpec((tm, tn), lambda i, j, k: (i, j)),
        scratch_shapes=[pltpu.VMEM((tm, tn), F32)],
        compiler_params=_params("arbitrary", "arbitrary", "arbitrary"),
        name=name,
    )(x, w)


def _layer_norm_rows(v, g, b):
    mu = jnp.mean(v, -1, keepdims=True)
    c = v - mu
    var = jnp.mean(c * c, -1, keepdims=True)
    return c * lax.rsqrt(var + LN_EPS) * g + b


def _ln_kernel(x_ref, f_ref, g_ref, b_ref, o_ref, ob_ref):
    v = DEEPNORM_ALPHA * x_ref[...] + f_ref[...].astype(F32)
    y = _layer_norm_rows(v, g_ref[...], b_ref[...])
    o_ref[...] = y
    ob_ref[...] = y.astype(BF16)


def residual_ln(x, f, g, b, *, tm=256):
    rows, width = x.shape
    row = pl.BlockSpec((tm, width), lambda i: (i, 0))
    vec = pl.BlockSpec((1, width), lambda i: (0, 0))
    return pl.pallas_call(
        _ln_kernel,
        out_shape=(jax.ShapeDtypeStruct((rows, width), F32), jax.ShapeDtypeStruct((rows, width), BF16)),
        grid=(rows // tm,),
        in_specs=[row, row, vec, vec],
        out_specs=(row, row),
        compiler_params=_params("arbitrary"),
        name="residual_ln",
    )(x, f, g.reshape(1, width), b.reshape(1, width))


def _rms(v, g):
    return v * lax.rsqrt(jnp.mean(v * v, -1, keepdims=True) + RMS_EPS) * g


def _mla_norm_kernel(cq_ref, ckv_ref, gq_ref, gkv_ref, oq_ref, okv_ref):
    oq_ref[...] = _rms(cq_ref[...], gq_ref[...]).astype(BF16)
    okv_ref[...] = _rms(ckv_ref[...], gkv_ref[...])


def mla_norms(h, q_norm, kv_norm, *, tm=512):
    rows = h.shape[0]
    q_blk = S5_WIDTH // MLA_Q_RANK
    kv_blk = (S5_WIDTH + MLA_Q_RANK) // MLA_KV_RANK
    return pl.pallas_call(
        _mla_norm_kernel,
        out_shape=(jax.ShapeDtypeStruct((rows, MLA_Q_RANK), BF16),
                   jax.ShapeDtypeStruct((rows, MLA_KV_RANK), F32)),
        grid=(rows // tm,),
        in_specs=[pl.BlockSpec((tm, MLA_Q_RANK), lambda i: (i, q_blk)),
                  pl.BlockSpec((tm, MLA_KV_RANK), lambda i: (i, kv_blk)),
                  pl.BlockSpec((1, MLA_Q_RANK), lambda i: (0, 0)),
                  pl.BlockSpec((1, MLA_KV_RANK), lambda i: (0, 0))],
        out_specs=(pl.BlockSpec((tm, MLA_Q_RANK), lambda i: (i, 0)),
                   pl.BlockSpec((tm, MLA_KV_RANK), lambda i: (i, 0))),
        compiler_params=_params("arbitrary"),
        name="mla_norms",
    )(h, h, q_norm.reshape(1, -1), kv_norm.reshape(1, -1))


def _rope_pair_kernel(x_ref, xr_ref, cos_ref, sin_ref, o_ref, *, hd):
    c, s = cos_ref[...], sin_ref[...]
    for h in range(x_ref.shape[1] // hd):
        sl = slice(h * hd, (h + 1) * hd)
        o_ref[:, sl] = (x_ref[:, sl] * c + xr_ref[:, sl] * s).astype(o_ref.dtype)


def rope_pair(src_x, x_blk, src_r, r_blk, width, cos, sin, *, out_dtype, tm=512):
    rows = src_x.shape[0]
    hd = cos.shape[1]
    return pl.pallas_call(
        functools.partial(_rope_pair_kernel, hd=hd),
        out_shape=jax.ShapeDtypeStruct((rows, width), out_dtype),
        grid=(rows // tm,),
        in_specs=[pl.BlockSpec((tm, width), lambda i: (i, x_blk)),
                  pl.BlockSpec((tm, width), lambda i: (i, r_blk)),
                  pl.BlockSpec((tm, hd), lambda i: (i, 0)),
                  pl.BlockSpec((tm, hd), lambda i: (i, 0))],
        out_specs=pl.BlockSpec((tm, width), lambda i: (i, 0)),
        compiler_params=_params("arbitrary"),
        name="rope_pair",
    )(src_x, src_r, cos, sin)


def _rope_roll_kernel(x_ref, cos_ref, sin_ref, o_ref):
    c, s = cos_ref[...], sin_ref[...]
    for h in range(x_ref.shape[1] // LANES):
        sl = slice(h * LANES, (h + 1) * LANES)
        x = x_ref[:, sl]
        o_ref[:, sl] = x * c + pltpu.roll(x, LANES // 2, 1) * s


def rope_roll(src, width, cos, sin_signed, *, tm=256):
    rows = src.shape[0]
    return pl.pallas_call(
        _rope_roll_kernel,
        out_shape=jax.ShapeDtypeStruct((rows, width), F32),
        grid=(rows // tm,),
        in_specs=[pl.BlockSpec((tm, width), lambda i: (i, 0)),
                  pl.BlockSpec((tm, LANES), lambda i: (i, 0)),
                  pl.BlockSpec((tm, LANES), lambda i: (i, 0))],
        out_specs=pl.BlockSpec((tm, width), lambda i: (i, 0)),
        compiler_params=_params("arbitrary"),
        name="rope_roll",
    )(src, cos, sin_signed)


def _s5_param_kernel(lr_ref, li_ref, ldt_ref, bre_ref, bim_ref, pre_ref, pim_ref, bbre_ref, bbim_ref):
    lr, li = lr_ref[...], li_ref[...]
    dt = jnp.exp(ldt_ref[...])
    for k in range(SUBLANES):
        decay = jnp.exp(lr * dt * (k + 1.0))
        pre_ref[k] = decay * jnp.cos(li * dt * (k + 1.0))
        pim_ref[k] = decay * jnp.sin(li * dt * (k + 1.0))
    a_re, a_im = pre_ref[0], pim_ref[0]
    inv_den = 1.0 / (lr * lr + li * li)
    f_re = ((a_re - 1.0) * lr + a_im * li) * inv_den
    f_im = (a_im * lr - (a_re - 1.0) * li) * inv_den
    for c in range(S5_GROUP):
        br, bi = bre_ref[c], bim_ref[c]
        bbre_ref[c] = f_re * br - f_im * bi
        bbim_ref[c] = f_re * bi + f_im * br


def s5_params(lam_re, lam_im, log_dt, b_re, b_im):
    g, p = lam_re.shape
    b_re_t = jnp.transpose(b_re, (2, 0, 1))
    b_im_t = jnp.transpose(b_im, (2, 0, 1))
    return pl.pallas_call(
        _s5_param_kernel,
        out_shape=(jax.ShapeDtypeStruct((SUBLANES, g, p), F32), jax.ShapeDtypeStruct((SUBLANES, g, p), F32),
                   jax.ShapeDtypeStruct((S5_GROUP, g, p), F32), jax.ShapeDtypeStruct((S5_GROUP, g, p), F32)),
        name="s5_params",
    )(lam_re, lam_im, log_dt.reshape(g, 1), b_re_t, b_im_t)


def _block_diag(blocks):
    s, n, r, c = blocks.shape
    eye = jnp.eye(n, dtype=blocks.dtype)
    return (blocks[:, :, :, None, :] * eye[None, :, None, :, None]).reshape(s, n * r, n * c)


def _gelu_tanh(y):
    return 0.5 * y * (1.0 + jnp.tanh(math.sqrt(2.0 / math.pi) * (y + 0.044715 * (y * y * y))))


S5_LANE_CHUNK = 512


def _s5_kernel(u_ref, h0re_ref, h0im_ref, pre_ref, pim_ref, wb_ref, wcre_ref, wcim_ref, d_ref,
               y_ref, hre_ref, him_ref, sre_ref, sim_ref, cre_ref, cim_ref, *, reset):
    rows = u_ref.shape[0]
    n_groups = rows // SUBLANES
    sup_u = S5_SUPER * S5_GROUP
    sup_h = S5_SUPER * S5_STATE

    for s in range(S5_NSUPER):
        ub = u_ref[:, s * sup_u:(s + 1) * sup_u].astype(BF16)
        bu = _dot(ub, wb_ref[s])
        sre_ref[:, s * sup_h:(s + 1) * sup_h] = bu[:, :sup_h]
        sim_ref[:, s * sup_h:(s + 1) * sup_h] = bu[:, sup_h:]

    if not reset:
        @pl.when(pl.program_id(1) == 0)
        def _():
            cre_ref[...] = h0re_ref[...]
            cim_ref[...] = h0im_ref[...]

    row_id = lax.broadcasted_iota(I32, (SUBLANES, S5_LANE_CHUNK), 0)
    for c0 in range(0, S5_FLAT, S5_LANE_CHUNK):
        lanes = slice(c0, c0 + S5_LANE_CHUNK)
        p_re, p_im = pre_ref[:, lanes], pim_ref[:, lanes]

        def body(gi, carry):
            car_re, car_im = carry
            r0 = pl.multiple_of(gi * SUBLANES, SUBLANES)
            x_re = sre_ref[pl.ds(r0, SUBLANES), lanes]
            x_im = sim_ref[pl.ds(r0, SUBLANES), lanes]
            for lvl, d in enumerate((1, 2, 4)):
                a_re = p_re[d - 1:d, :]
                a_im = p_im[d - 1:d, :]
                keep = row_id >= d
                s_re = jnp.where(keep, pltpu.roll(x_re, d, 0), 0.0)
                s_im = jnp.where(keep, pltpu.roll(x_im, d, 0), 0.0)
                x_re, x_im = (x_re + a_re * s_re - a_im * s_im,
                              x_im + a_re * s_im + a_im * s_re)
            if reset:
                car_re = h0re_ref[pl.ds(gi, 1), lanes]
                car_im = h0im_ref[pl.ds(gi, 1), lanes]
            h_re = x_re + p_re * car_re - p_im * car_im
            h_im = x_im + p_re * car_im + p_im * car_re
            sre_ref[pl.ds(r0, SUBLANES), lanes] = h_re
            sim_ref[pl.ds(r0, SUBLANES), lanes] = h_im
            last_re = h_re[SUBLANES - 1:SUBLANES, :]
            last_im = h_im[SUBLANES - 1:SUBLANES, :]
            if reset:
                hre_ref[pl.ds(gi, 1), lanes] = last_re
                him_ref[pl.ds(gi, 1), lanes] = last_im
            return last_re, last_im

        if reset:
            init = (jnp.zeros((1, S5_LANE_CHUNK), F32), jnp.zeros((1, S5_LANE_CHUNK), F32))
        else:
            init = (cre_ref[:, lanes], cim_ref[:, lanes])
        fin_re, fin_im = lax.fori_loop(0, n_groups, body, init)
        if not reset:
            cre_ref[:, lanes] = fin_re
            cim_ref[:, lanes] = fin_im
            hre_ref[:, lanes] = fin_re
            him_ref[:, lanes] = fin_im

    for s in range(S5_NSUPER):
        h_re = sre_ref[:, s * sup_h:(s + 1) * sup_h].astype(BF16)
        h_im = sim_ref[:, s * sup_h:(s + 1) * sup_h].astype(BF16)
        cols = slice(s * sup_u, (s + 1) * sup_u)
        y = _dot(h_re, wcre_ref[s]) + _dot(h_im, wcim_ref[s]) + d_ref[:, cols] * u_ref[:, cols]
        y_ref[:, cols] = _gelu_tanh(y)


def s5_core(h_in, row_block_offset, n_seq, seq_len, h0_re, h0_im, consts, *, reset, tile_rows):
    pow_re, pow_im, wb, wc_re, wc_im, d_skip = consts
    rows = n_seq * seq_len
    const2 = lambda *_: (0, 0)
    const3 = lambda *_: (0, 0, 0)
    w_specs = [pl.BlockSpec(pow_re.shape, const2), pl.BlockSpec(pow_im.shape, const2),
               pl.BlockSpec(wb.shape, const3), pl.BlockSpec(wc_re.shape, const3),
               pl.BlockSpec(wc_im.shape, const3), pl.BlockSpec(d_skip.shape, const2)]
    scratch = [pltpu.VMEM((tile_rows, S5_FLAT), F32), pltpu.VMEM((tile_rows, S5_FLAT), F32),
               pltpu.VMEM((1, S5_FLAT), F32), pltpu.VMEM((1, S5_FLAT), F32)]
    kern = functools.partial(_s5_kernel, reset=reset)
    if reset:
        assert seq_len == SUBLANES and rows % tile_rows == 0
        seqs = tile_rows // SUBLANES
        grid = (rows // tile_rows,)
        u_spec = pl.BlockSpec((tile_rows, S5_WIDTH), lambda i: (i + row_block_offset, 0))
        h_spec = pl.BlockSpec((seqs, S5_FLAT), lambda i: (i, 0))
        y_spec = pl.BlockSpec((tile_rows, S5_WIDTH), lambda i: (i, 0))
        st_shape = jax.ShapeDtypeStruct((n_seq, S5_FLAT), F32)
        sem = ("arbitrary",)
    else:
        assert seq_len % tile_rows == 0
        per = seq_len // tile_rows
        grid = (n_seq, per)
        u_spec = pl.BlockSpec((tile_rows, S5_WIDTH), lambda b, i: (b * per + i + row_block_offset, 0))
        h_spec = pl.BlockSpec((None, 1, S5_FLAT), lambda b, i: (b, 0, 0))
        y_spec = pl.BlockSpec((tile_rows, S5_WIDTH), lambda b, i: (b * per + i, 0))
        st_shape = jax.ShapeDtypeStruct((n_seq, 1, S5_FLAT), F32)
        h0_re = h0_re.reshape(n_seq, 1, S5_FLAT)
        h0_im = h0_im.reshape(n_seq, 1, S5_FLAT)
        sem = ("arbitrary", "arbitrary")
    y, st_re, st_im = pl.pallas_call(
        kern,
        out_shape=(jax.ShapeDtypeStruct((rows, S5_WIDTH), F32), st_shape, st_shape),
        grid=grid,
        in_specs=[u_spec, h_spec, h_spec] + w_specs,
        out_specs=(y_spec, h_spec, h_spec),
        scratch_shapes=scratch,
        compiler_params=_params(*sem),
        name="s5_reset" if reset else "s5_scan",
    )(h_in, h0_re, h0_im, pow_re, pow_im, wb, wc_re, wc_im, d_skip)
    return y, st_re.reshape(n_seq, S5_FLAT), st_im.reshape(n_seq, S5_FLAT)


def s5_constants(lam_re, lam_im, log_dt, b_re, b_im, c_re, c_im, d_skip):
    pow_re, pow_im, bb_re, bb_im = s5_params(lam_re, lam_im, log_dt, b_re, b_im)
    pow_re = pow_re.reshape(SUBLANES, S5_FLAT)
    pow_im = pow_im.reshape(SUBLANES, S5_FLAT)
    def to_blocks(bb):
        return jnp.transpose(bb, (1, 0, 2)).reshape(S5_NSUPER, S5_SUPER, S5_GROUP, S5_STATE)
    wb = jnp.concatenate([_block_diag(to_blocks(bb_re)), _block_diag(to_blocks(bb_im))], -1).astype(BF16)
    def c_blocks(c):
        return jnp.transpose(c, (0, 2, 1)).reshape(S5_NSUPER, S5_SUPER, S5_STATE, S5_GROUP)
    wc_re = _block_diag(c_blocks(c_re)).astype(BF16)
    wc_im = _block_diag(c_blocks(-c_im)).astype(BF16)
    return pow_re, pow_im, wb, wc_re, wc_im, d_skip.reshape(1, S5_WIDTH)


def _flash_kernel(qn_ref, qp_ref, kn_ref, kp_ref, v_ref, o_ref, m_ref, l_ref, acc_ref, *, scale):
    i, j = pl.program_id(2), pl.program_id(3)
    tq, tk = qn_ref.shape[0], kn_ref.shape[0]

    @pl.when(j == 0)
    def _():
        m_ref[...] = jnp.full_like(m_ref, NEG_INF)
        l_ref[...] = jnp.zeros_like(l_ref)
        acc_ref[...] = jnp.zeros_like(acc_ref)

    @pl.when(j <= i)
    def _():
        q = jnp.concatenate([qn_ref[...].astype(BF16), qp_ref[...].astype(BF16)], -1)
        k = jnp.concatenate([kn_ref[...].astype(BF16), kp_ref[...].astype(BF16)], -1)
        s = _dot_nt(q, k) * scale
        qpos = i * tq + lax.broadcasted_iota(I32, (tq, tk), 0)
        kpos = j * tk + lax.broadcasted_iota(I32, (tq, tk), 1)
        s = jnp.where(kpos <= qpos, s, NEG_INF)
        m_new = jnp.maximum(m_ref[...], jnp.max(s, -1, keepdims=True))
        corr = jnp.exp(m_ref[...] - m_new)
        p = jnp.exp(s - m_new)
        l_ref[...] = corr * l_ref[...] + jnp.sum(p, -1, keepdims=True)
        acc_ref[...] = corr * acc_ref[...] + _dot(p.astype(BF16), v_ref[...].astype(BF16))
        m_ref[...] = m_new

    @pl.when(j == pl.num_programs(3) - 1)
    def _():
        o_ref[...] = (acc_ref[...] / l_ref[...]).astype(o_ref.dtype)


def mla_prompt_attention(q_ext, q_pe, k_nope, k_pe, v, n_seq, seq_len, *, scale, tq=512, tk=512):
    nq, nk = seq_len // tq, seq_len // tk
    hd = LANES

    def qmap(b, h, i, j):
        return (b * nq + i, h)

    def kmap(b, h, i, j):
        return (b * nk + jnp.minimum(j, i), h)

    def kpmap(b, h, i, j):
        return (b * nk + jnp.minimum(j, i), 0)

    return pl.pallas_call(
        functools.partial(_flash_kernel, scale=scale),
        out_shape=jax.ShapeDtypeStruct((n_seq * seq_len, MLA_HEADS * hd), BF16),
        grid=(n_seq, MLA_HEADS, nq, nk),
        in_specs=[pl.BlockSpec((tq, hd), qmap), pl.BlockSpec((tq, hd), qmap),
                  pl.BlockSpec((tk, hd), kmap), pl.BlockSpec((tk, hd), kpmap),
                  pl.BlockSpec((tk, hd), kmap)],
        out_specs=pl.BlockSpec((tq, hd), qmap),
        scratch_shapes=[pltpu.VMEM((tq, 1), F32), pltpu.VMEM((tq, 1), F32), pltpu.VMEM((tq, hd), F32)],
        compiler_params=_params("arbitrary", "arbitrary", "arbitrary", "arbitrary"),
        name="mla_prompt_attention",
    )(q_ext, q_pe, k_nope, k_pe, v)


PAGES_PER_STEP = 8


def _paged_kernel(pt_ref, ql_ref, qp_ref, cn_ref, pn_ref, *refs, scale, heads):
    lat_refs = refs[:PAGES_PER_STEP]
    pe_refs = refs[PAGES_PER_STEP:2 * PAGES_PER_STEP]
    o_ref, m_ref, l_ref, acc_ref = refs[2 * PAGES_PER_STEP:]
    j = pl.program_id(1)
    rows = ql_ref.shape[0]

    @pl.when(j == 0)
    def _():
        m_ref[...] = jnp.full_like(m_ref, NEG_INF)
        l_ref[...] = jnp.zeros_like(l_ref)
        acc_ref[...] = jnp.zeros_like(acc_ref)

    ql, qp = ql_ref[...], qp_ref[...]
    lats = [r[...].astype(BF16) for r in lat_refs]
    s = jnp.concatenate(
        [_dot_nt(ql, lat) + _dot_nt(qp, pe[...].astype(BF16)) for lat, pe in zip(lats, pe_refs)], -1) * scale
    m_new = jnp.maximum(m_ref[...], jnp.max(s, -1, keepdims=True))
    corr = jnp.exp(m_ref[...] - m_new)
    p = jnp.exp(s - m_new)
    l_ref[...] = corr * l_ref[...] + jnp.sum(p, -1, keepdims=True)
    pv = _dot(p[:, :PAGE_SIZE].astype(BF16), lats[0])
    for n in range(1, PAGES_PER_STEP):
        pv += _dot(p[:, n * PAGE_SIZE:(n + 1) * PAGE_SIZE].astype(BF16), lats[n])
    acc_ref[...] = corr * acc_ref[...] + pv
    m_ref[...] = m_new

    @pl.when(j == pl.num_programs(1) - 1)
    def _():
        cn = cn_ref[...].astype(BF16)
        t_new = cn.shape[0]
        sn = (_dot_nt(ql, cn) + _dot_nt(qp, pn_ref[...].astype(BF16))) * scale
        q_tok = lax.broadcasted_iota(I32, (rows, t_new), 0) // heads
        k_tok = lax.broadcasted_iota(I32, (rows, t_new), 1)
        sn = jnp.where(k_tok <= q_tok, sn, NEG_INF)
        m_fin = jnp.maximum(m_ref[...], jnp.max(sn, -1, keepdims=True))
        corr2 = jnp.exp(m_ref[...] - m_fin)
        pn = jnp.exp(sn - m_fin)
        l_fin = corr2 * l_ref[...] + jnp.sum(pn, -1, keepdims=True)
        acc = corr2 * acc_ref[...] + _dot(pn.astype(BF16), cn)
        o_ref[...] = (acc / l_fin).astype(o_ref.dtype)


def mla_sample_attention(page_table, q_lat, q_pe, c_new, pe_new, cache_lat, cache_pe, layer, *, scale):
    n_seq, rows, rank = q_lat.shape
    n_pages = page_table.shape[1]
    assert n_pages % PAGES_PER_STEP == 0
    steps = n_pages // PAGES_PER_STEP
    t_new = c_new.shape[1]
    rope = q_pe.shape[2]

    def seq_map(b, j, pt):
        return (b, 0, 0)

    def page_map(n):
        return lambda b, j, pt: (layer, pt[b * n_pages + j * PAGES_PER_STEP + n], 0, 0)

    in_specs = [pl.BlockSpec((None, rows, rank), seq_map), pl.BlockSpec((None, rows, rope), seq_map),
                pl.BlockSpec((None, t_new, rank), seq_map), pl.BlockSpec((None, t_new, rope), seq_map)]
    in_specs += [pl.BlockSpec((None, None, PAGE_SIZE, rank), page_map(n)) for n in range(PAGES_PER_STEP)]
    in_specs += [pl.BlockSpec((None, None, PAGE_SIZE, rope), page_map(n)) for n in range(PAGES_PER_STEP)]
    return pl.pallas_call(
        functools.partial(_paged_kernel, scale=scale, heads=rows // t_new),
        out_shape=jax.ShapeDtypeStruct((n_seq, rows, rank), BF16),
        grid_spec=pltpu.PrefetchScalarGridSpec(
            num_scalar_prefetch=1, grid=(n_seq, steps), in_specs=in_specs,
            out_specs=pl.BlockSpec((None, rows, rank), seq_map),
            scratch_shapes=[pltpu.VMEM((rows, 1), F32), pltpu.VMEM((rows, 1), F32),
                            pltpu.VMEM((rows, rank), F32)]),
        compiler_params=_params("arbitrary", "arbitrary"),
        name="mla_sample_attention",
    )(page_table.reshape(-1), q_lat, q_pe, c_new, pe_new,
      *([cache_lat] * PAGES_PER_STEP), *([cache_pe] * PAGES_PER_STEP))


def _per_head_kernel(x_ref, w_ref, o_ref, *, transpose_w):
    x = x_ref[...].astype(BF16)
    w = w_ref[...].astype(BF16)
    o_ref[...] = (_dot_nt(x, w) if transpose_w else _dot(x, w)).astype(o_ref.dtype)


def per_head_matmul(x, x_row_blk, rows, in_w, w2d, out_w, *, transpose_w, out_dtype):
    if transpose_w:
        w_spec = pl.BlockSpec((out_w, in_w), lambda h: (0, h))
    else:
        w_spec = pl.BlockSpec((in_w, out_w), lambda h: (0, h))
    return pl.pallas_call(
        functools.partial(_per_head_kernel, transpose_w=transpose_w),
        out_shape=jax.ShapeDtypeStruct((rows, MLA_HEADS * out_w), out_dtype),
        grid=(MLA_HEADS,),
        in_specs=[pl.BlockSpec((rows, in_w), lambda h: (x_row_blk, h)), w_spec],
        out_specs=pl.BlockSpec((rows, out_w), lambda h: (0, h)),
        compiler_params=_params("arbitrary"),
        name="per_head_matmul",
    )(x, w2d)


def _cross_kernel(q_ref, k_ref, v_ref, o_ref, *, scale):
    for h in range(MEM_HEADS):
        sl = slice(h * MEM_HEAD_DIM, (h + 1) * MEM_HEAD_DIM)
        s = _dot_nt(q_ref[:, sl].astype(BF16), k_ref[:, sl].astype(BF16)) * scale
        p = jnp.exp(s - jnp.max(s, -1, keepdims=True))
        p = p / jnp.sum(p, -1, keepdims=True)
        o_ref[:, sl] = _dot(p.astype(BF16), v_ref[:, sl].astype(BF16))


def cross_attention(q, q_row_blk_offset, n_seq, seq_len, k, v, kv_map, kv_block, *, tq):
    per = seq_len // tq
    q_spec = pl.BlockSpec((tq, MEM_WIDTH), lambda b, i: (b * per + i + q_row_blk_offset, 0))
    kv_spec = pl.BlockSpec(kv_block, kv_map)
    return pl.pallas_call(
        functools.partial(_cross_kernel, scale=MEM_SCALE),
        out_shape=jax.ShapeDtypeStruct((n_seq * seq_len, MEM_WIDTH), F32),
        grid=(n_seq, per),
        in_specs=[q_spec, kv_spec, kv_spec],
        out_specs=pl.BlockSpec((tq, MEM_WIDTH), lambda b, i: (b * per + i, 0)),
        compiler_params=_params("arbitrary", "arbitrary"),
        name="cross_attention",
    )(q, k, v)


def _window_kernel(q_ref, kp_ref, kc_ref, vp_ref, vc_ref, o_ref, lse_ref, *, scale):
    i = pl.program_id(1)
    t = q_ref.shape[0]
    qpos = i * t + lax.broadcasted_iota(I32, (t, 2 * t), 0)
    kpos = (i - 1) * t + lax.broadcasted_iota(I32, (t, 2 * t), 1)
    valid = (kpos >= 0) & (kpos <= qpos) & (kpos >= qpos - C_SPAN)
    for kvh in range(C_KV_HEADS):
        ksl = slice(kvh * C_HEAD_DIM, (kvh + 1) * C_HEAD_DIM)
        k = jnp.concatenate([kp_ref[:, ksl], kc_ref[:, ksl]], 0).astype(BF16)
        v = jnp.concatenate([vp_ref[:, ksl], vc_ref[:, ksl]], 0).astype(BF16)
        for r in range(C_REP):
            slot = kvh * C_REP + r
            sl = slice(slot * C_HEAD_DIM, (slot + 1) * C_HEAD_DIM)
            s = _dot_nt(q_ref[:, sl].astype(BF16), k) * scale
            s = jnp.where(valid, s, NEG_INF)
            m = jnp.max(s, -1, keepdims=True)
            e = jnp.exp(s - m)
            den = jnp.sum(e, -1, keepdims=True)
            o_ref[:, sl] = _dot(e.astype(BF16), v) / den
            lse_ref[:, sl] = jnp.broadcast_to(m + jnp.log(den), (t, C_HEAD_DIM))


def window_attention(q, k, v):
    n_cls, t_len, _ = q.shape
    t = C_SPAN
    cur = lambda c, i: (c, i, 0)
    prev = lambda c, i: (c, jnp.maximum(i - 1, 0), 0)
    q_spec = pl.BlockSpec((None, t, C_OUT_WIDTH), cur)
    shape = jax.ShapeDtypeStruct((n_cls, t_len, C_OUT_WIDTH), F32)
    return pl.pallas_call(
        functools.partial(_window_kernel, scale=C_SCALE),
        out_shape=(shape, shape),
        grid=(n_cls, t_len // t),
        in_specs=[q_spec, pl.BlockSpec((None, t, C_KV_WIDTH), prev), pl.BlockSpec((None, t, C_KV_WIDTH), cur),
                  pl.BlockSpec((None, t, C_KV_WIDTH), prev), pl.BlockSpec((None, t, C_KV_WIDTH), cur)],
        out_specs=(q_spec, q_spec),
        compiler_params=_params("arbitrary", "arbitrary"),
        name="window_attention",
    )(q, k, k, v, v)


def _merge_kernel(*refs):
    n = C_GROUPS
    o_refs, l_refs, out_ref = refs[:n], refs[n:2 * n], refs[2 * n]
    lses = [r[...] for r in l_refs]
    m = functools.reduce(jnp.maximum, lses)
    ws = [jnp.exp(l - m) for l in lses]
    tot = functools.reduce(lambda a, b: a + b, ws)
    acc = ws[0] * o_refs[0][...]
    for w, o in zip(ws[1:], o_refs[1:]):
        acc += w * o[...]
    out_ref[...] = (acc / tot).astype(out_ref.dtype)


def merge_groups(outs, lses, *, tm=256):
    rows, width = outs[0].shape
    spec = pl.BlockSpec((tm, width), lambda i: (i, 0))
    return pl.pallas_call(
        _merge_kernel,
        out_shape=jax.ShapeDtypeStruct((rows, width), BF16),
        grid=(rows // tm,),
        in_specs=[spec] * (2 * C_GROUPS),
        out_specs=spec,
        compiler_params=_params("arbitrary"),
        name="merge_groups",
    )(*outs, *lses)


def _dilated_sample_kernel(q_ref, kn_ref, vn_ref, bk_ref, bv_ref, o_ref, nk_ref, nv_ref, *, scale):
    t_new = q_ref.shape[0]
    buf = bk_ref.shape[0]
    rows = C_GROUPS * C_REP * t_new
    per_group = C_REP * t_new

    def row_consts(n_keys):
        rid = lax.broadcasted_iota(I32, (rows, n_keys), 0)
        grp = rid // per_group
        dil = jnp.where(grp == 0, C_PATTERNS[0][1], jnp.where(grp == 1, C_PATTERNS[1][1], C_PATTERNS[2][1]))
        win = jnp.where(grp == 0, C_PATTERNS[0][0], jnp.where(grp == 1, C_PATTERNS[1][0], C_PATTERNS[2][0]))
        return rid % t_new, dil, win

    tok_b, dil_b, win_b = row_consts(buf)
    rel_b = buf + tok_b - lax.broadcasted_iota(I32, (rows, buf), 1)
    valid_b = ((rel_b & (dil_b - 1)) == 0) & (rel_b <= win_b)
    tok_n, dil_n, _ = row_consts(t_new)
    rel_n = tok_n - lax.broadcasted_iota(I32, (rows, t_new), 1)
    valid_n = (rel_n >= 0) & ((rel_n & (dil_n - 1)) == 0)

    for kvh in range(C_KV_HEADS):
        ksl = slice(kvh * C_HEAD_DIM, (kvh + 1) * C_HEAD_DIM)
        pieces = []
        for g in range(C_GROUPS):
            for r in range(C_REP):
                col = (g * C_SLOTS + kvh * C_REP + r) * C_HEAD_DIM
                pieces.append(q_ref[:, col:col + C_HEAD_DIM])
        q = jnp.concatenate(pieces, 0).astype(BF16)
        kb, vb = bk_ref[:, ksl].astype(BF16), bv_ref[:, ksl].astype(BF16)
        kn, vn = kn_ref[:, ksl].astype(BF16), vn_ref[:, ksl].astype(BF16)
        s_b = jnp.where(valid_b, _dot_nt(q, kb) * scale, NEG_INF)
        s_n = jnp.where(valid_n, _dot_nt(q, kn) * scale, NEG_INF)
        m = jnp.maximum(jnp.max(s_b, -1, keepdims=True), jnp.max(s_n, -1, keepdims=True))
        e_b, e_n = jnp.exp(s_b - m), jnp.exp(s_n - m)
        den = jnp.sum(e_b, -1, keepdims=True) + jnp.sum(e_n, -1, keepdims=True)
        out = (_dot(e_b.astype(BF16), vb) + _dot(e_n.astype(BF16), vn)) / den
        lse = m + jnp.log(den)
        lg = [lse[g * per_group:(g + 1) * per_group] for g in range(C_GROUPS)]
        mm_ = functools.reduce(jnp.maximum, lg)
        wg = [jnp.exp(l - mm_) for l in lg]
        tot = functools.reduce(lambda a, b: a + b, wg)
        merged = sum(w * out[g * per_group:(g + 1) * per_group] for g, w in enumerate(wg)) / tot
        for r in range(C_REP):
            col = (kvh * C_REP + r) * C_HEAD_DIM
            o_ref[:, col:col + C_HEAD_DIM] = merged[r * t_new:(r + 1) * t_new]

    nk_ref[0:buf - t_new, :] = bk_ref[t_new:buf, :]
    nk_ref[buf - t_new:buf, :] = kn_ref[...]
    nv_ref[0:buf - t_new, :] = bv_ref[t_new:buf, :]
    nv_ref[buf - t_new:buf, :] = vn_ref[...]


def dilated_sample(qk, h, row_blk_offset, n_seq, t_new, cache_k, cache_v, layer):
    buf = cache_k.shape[2]
    assert t_new == SUBLANES
    row = lambda b: (b + row_blk_offset, 0)
    cache_spec = pl.BlockSpec((None, None, buf, C_KV_WIDTH), lambda b: (layer, b, 0, 0))
    new_spec = pl.BlockSpec((None, buf, C_KV_WIDTH), lambda b: (b, 0, 0))
    new_shape = jax.ShapeDtypeStruct((n_seq, buf, C_KV_WIDTH), F32)
    return pl.pallas_call(
        functools.partial(_dilated_sample_kernel, scale=C_SCALE),
        out_shape=(jax.ShapeDtypeStruct((n_seq * t_new, C_OUT_WIDTH), F32), new_shape, new_shape),
        grid=(n_seq,),
        in_specs=[pl.BlockSpec((t_new, C_Q_WIDTH), row),
                  pl.BlockSpec((t_new, C_KV_WIDTH), lambda b: (b + row_blk_offset, C_Q_WIDTH // C_KV_WIDTH)),
                  pl.BlockSpec((t_new, C_KV_WIDTH),
                               lambda b: (b + row_blk_offset, (C_Q_WIDTH + C_KV_WIDTH) // C_KV_WIDTH)),
                  cache_spec, cache_spec],
        out_specs=(pl.BlockSpec((t_new, C_OUT_WIDTH), lambda b: (b, 0)), new_spec, new_spec),
        compiler_params=_params("arbitrary"),
        name="dilated_sample",
    )(qk, qk, h, cache_k, cache_v)


def _router_kernel(x_ref, w_ref, o_ref):
    logits = jnp.dot(x_ref[...], w_ref[...], precision=lax.Precision.HIGHEST, preferred_element_type=F32)
    lane = lax.broadcasted_iota(I32, logits.shape, 1)
    m1 = jnp.max(logits, -1, keepdims=True)
    i1 = jnp.min(jnp.where(logits == m1, lane, N_EXPERTS), -1, keepdims=True)
    rest = jnp.where(lane == i1, NEG_INF, logits)
    m2 = jnp.max(rest, -1, keepdims=True)
    i2 = jnp.min(jnp.where(rest == m2, lane, N_EXPERTS), -1, keepdims=True)
    e2 = jnp.exp(m2 - m1)
    g1 = 1.0 / (1.0 + e2)
    g2 = e2 / (1.0 + e2)
    o_ref[...] = jnp.where(lane == 0, i1.astype(F32),
                           jnp.where(lane == 1, i2.astype(F32),
                                     jnp.where(lane == 2, g1, jnp.where(lane == 3, g2, 0.0))))


def router_top2(x, w_router, *, tm=512):
    rows, width = x.shape
    return pl.pallas_call(
        _router_kernel,
        out_shape=jax.ShapeDtypeStruct((rows, N_EXPERTS), F32),
        grid=(rows // tm,),
        in_specs=[pl.BlockSpec((tm, width), lambda i: (i, 0)),
                  pl.BlockSpec((width, N_EXPERTS), lambda i: (0, 0))],
        out_specs=pl.BlockSpec((tm, N_EXPERTS), lambda i: (i, 0)),
        compiler_params=_params("arbitrary"),
        name="router_top2",
    )(x, w_router)


def _row_copy(src_ref, src_row, dst_ref, dst_row, sem):
    return pltpu.make_async_copy(src_ref.at[pl.ds(src_row, 1)], dst_ref.at[pl.ds(dst_row, 1)], sem)


def _gather_rows(idx_ref, base, n, src_ref, dst_ref, sem):
    def start(r, c):
        _row_copy(src_ref, idx_ref[base + r], dst_ref, r, sem).start()
        return c

    lax.fori_loop(0, n, start, 0)

    def wait(r, c):
        _row_copy(src_ref, 0, dst_ref, r, sem).wait()
        return c

    lax.fori_loop(0, n, wait, 0)


def _sort_gather_kernel(idx_ref, x_ref, o_ref, buf_ref, sem):
    tm = o_ref.shape[0]
    _gather_rows(idx_ref, pl.program_id(0) * tm, tm, x_ref, buf_ref, sem)
    o_ref[...] = buf_ref[...].astype(o_ref.dtype)


def sort_gather(x, src_rows, *, tm=256):
    n_out = src_rows.shape[0]
    width = x.shape[1]
    return pl.pallas_call(
        _sort_gather_kernel,
        out_shape=jax.ShapeDtypeStruct((n_out, width), BF16),
        grid_spec=pltpu.PrefetchScalarGridSpec(
            num_scalar_prefetch=1, grid=(n_out // tm,),
            in_specs=[pl.BlockSpec(memory_space=pl.ANY)],
            out_specs=pl.BlockSpec((tm, width), lambda i, idx: (i, 0)),
            scratch_shapes=[pltpu.VMEM((tm, width), x.dtype), pltpu.SemaphoreType.DMA]),
        compiler_params=_params("arbitrary"),
        name="sort_gather",
    )(src_rows, x)


def _combine_ln_kernel(idx_ref, y_ref, x_ref, info_ref, g_ref, b_ref, o_ref, ob_ref, y0_ref, y1_ref, sem):
    tm = x_ref.shape[0]
    n_tok = pl.num_programs(0) * tm
    base = pl.program_id(0) * tm
    _gather_rows(idx_ref, base, tm, y_ref, y0_ref, sem)
    _gather_rows(idx_ref, n_tok + base, tm, y_ref, y1_ref, sem)
    info = info_ref[...]
    f = info[:, 2:3] * y0_ref[...] + info[:, 3:4] * y1_ref[...]
    y = _layer_norm_rows(DEEPNORM_ALPHA * x_ref[...] + f, g_ref[...], b_ref[...])
    o_ref[...] = y
    ob_ref[...] = y.astype(BF16)


def combine_ln(y_sorted, dest, x, info, g, b, *, tm=128):
    rows, width = x.shape
    row = pl.BlockSpec((tm, width), lambda i, idx: (i, 0))
    vec = pl.BlockSpec((1, width), lambda i, idx: (0, 0))
    return pl.pallas_call(
        _combine_ln_kernel,
        out_shape=(jax.ShapeDtypeStruct((rows, width), F32), jax.ShapeDtypeStruct((rows, width), BF16)),
        grid_spec=pltpu.PrefetchScalarGridSpec(
            num_scalar_prefetch=1, grid=(rows // tm,),
            in_specs=[pl.BlockSpec(memory_space=pl.ANY), row,
                      pl.BlockSpec((tm, N_EXPERTS), lambda i, idx: (i, 0)), vec, vec],
            out_specs=(row, row),
            scratch_shapes=[pltpu.VMEM((tm, width), F32), pltpu.VMEM((tm, width), F32),
                            pltpu.SemaphoreType.DMA]),
        compiler_params=_params("arbitrary"),
        name="combine_ln",
    )(dest, y_sorted, x, info, g.reshape(1, width), b.reshape(1, width))


MOE_TILE = 512


def moe_routing(info):
    n_tok = info.shape[0]
    n_asg = n_tok * TOP_K
    n_tiles = n_asg // MOE_TILE + N_EXPERTS
    expert = info[:, :TOP_K].astype(I32).T.reshape(-1)
    onehot = (expert[:, None] == jnp.arange(N_EXPERTS, dtype=I32)[None, :]).astype(I32)
    counts = jnp.sum(onehot, 0)
    rank = jnp.sum((jnp.cumsum(onehot, 0) - onehot) * onehot, 1)
    padded = ((counts + MOE_TILE - 1) // MOE_TILE) * MOE_TILE
    ends = jnp.cumsum(padded)
    dest = (ends - padded)[expert] + rank
    token = jnp.arange(n_asg, dtype=I32) % n_tok
    src_rows = jnp.zeros((n_tiles * MOE_TILE,), I32).at[dest].set(token)
    tile_start = jnp.arange(n_tiles, dtype=I32) * MOE_TILE
    tile_expert = jnp.minimum(jnp.sum((tile_start[:, None] >= ends[None, :]).astype(I32), 1), N_EXPERTS - 1)
    return src_rows, dest.astype(I32), tile_expert.astype(I32)


def moe_block(x, xb_unused, w_router, w_gate, w_up, w_down, g, b):
    info = router_top2(x, w_router)
    src_rows, dest, tile_expert = moe_routing(info)
    xs = sort_gather(x, src_rows)
    hs = gmm(xs, (w_gate, w_up), tile_expert, tm=MOE_TILE, tn=256, out_dtype=BF16, epilogue="swiglu",
             name="moe_up")
    ys = gmm(hs, (w_down,), tile_expert, tm=MOE_TILE, tn=512, out_dtype=F32, name="moe_down")
    return combine_ln(ys, dest, x, info, g, b)


def _rope_tables(pos, half):
    inv = ROPE_THETA ** (-jnp.arange(half, dtype=F32) / half)
    ang = pos.astype(F32)[:, None] * inv[None, :]
    return jnp.cos(ang), jnp.sin(ang)


def _rot_half_cols(w, hd):
    lead = w.shape[:-1]
    wh = w.reshape(lead + (-1, 2, hd // 2))
    return jnp.stack([-wh[..., 1, :], wh[..., 0, :]], -2).reshape(w.shape)


def _dense(x, w, *, tn, out_dtype, tm=1024, **kw):
    return gmm(x, (w,), tm=min(tm, x.shape[0]), tn=tn, out_dtype=out_dtype, **kw)


def kernel(x_prompt, x_sample, mem_prompt, cache_mla_lat, cache_mla_pe, cache_c_k, cache_c_v, cache_mem_k, cache_mem_v, state_s5_re, state_s5_im, page_table, ln_g, ln_b, w_in_even, s5_lambda_re, s5_lambda_im, s5_log_dt, s5_b_re, s5_b_im, s5_c_re, s5_c_im, s5_d, s5_w_glu, mla_q_norm, mla_kv_norm, mla_w_uq, mla_w_uk, mla_w_uv, w_out_even, w_in_odd, w_out_odd, mem_w_q, mem_w_k, mem_w_v, mem_w_o, ffn_w_gate, ffn_w_up, ffn_w_down, moe_w_router, moe_w_gate, moe_w_up, moe_w_down):
    bp, tp, d = x_prompt.shape
    bs, ts, _ = x_sample.shape
    n_p, n_s = bp * tp, bs * ts
    n_tok = n_p + n_s
    n_pages = page_table.shape[1]
    past_len = n_pages * PAGE_SIZE
    mem_len = mem_prompt.shape[1]
    c_buf = cache_c_k.shape[2]
    ROW = 1024
    assert n_p % ROW == 0 and n_s % ROW == 0 and ts == SUBLANES

    x = jnp.concatenate([x_prompt.reshape(n_p, d), x_sample.reshape(n_s, d)], 0)
    xb = x.astype(BF16)
    mem_b = mem_prompt.reshape(bp * mem_len, d)

    pos = jnp.concatenate([jnp.tile(jnp.arange(tp, dtype=I32), bp),
                           jnp.tile(past_len + jnp.arange(ts, dtype=I32), bs)])
    cos32, sin32 = _rope_tables(pos, MLA_ROPE // 2)
    zeros64 = jnp.zeros((n_tok, LANES - MLA_ROPE), F32)
    cos_mla = jnp.concatenate([cos32, cos32, zeros64], 1)
    sin_mla = jnp.concatenate([sin32, sin32, zeros64], 1)
    cos64, sin64 = _rope_tables(pos, C_HEAD_DIM // 2)
    cos_c = jnp.concatenate([cos64, cos64], 1)
    sin_c = jnp.concatenate([-sin64, sin64], 1)

    cache_ck = cache_c_k.reshape(cache_c_k.shape[0], bs, c_buf, C_KV_WIDTH)
    cache_cv = cache_c_v.reshape(cache_c_v.shape[0], bs, c_buf, C_KV_WIDTH)
    cache_mk = cache_mem_k.reshape(DEPTH, bs, mem_len, MEM_WIDTH)
    cache_mv = cache_mem_v.reshape(DEPTH, bs, mem_len, MEM_WIDTH)

    lat_p, pe_p, lat_s, pe_s = [], [], [], []
    s5r_p, s5i_p, s5r_s, s5i_s = [], [], [], []
    ck_p, cv_p, ck_s, cv_s = [], [], [], []
    mk_p, mv_p = [], []

    for l in range(DEPTH):
        if l % 2 == 0:
            e = l // 2
            h = _dense(xb, w_in_even[e], tn=512, out_dtype=F32, name="even_in")

            consts = s5_constants(s5_lambda_re[e], s5_lambda_im[e], s5_log_dt[e], s5_b_re[e], s5_b_im[e],
                                  s5_c_re[e], s5_c_im[e], s5_d[e])
            zero_state = jnp.zeros((bp, S5_FLAT), F32)
            y_p, hr_p, hi_p = s5_core(h, 0, bp, tp, zero_state, zero_state, consts, reset=False, tile_rows=128)
            y_s, hr_s, hi_s = s5_core(h, n_p // 128, bs, ts, state_s5_re[e].reshape(bs, S5_FLAT),
                                      state_s5_im[e].reshape(bs, S5_FLAT), consts, reset=True, tile_rows=128)
            s5r_p.append(hr_p.reshape(bp, S5_GROUPS, S5_STATE))
            s5i_p.append(hi_p.reshape(bp, S5_GROUPS, S5_STATE))
            s5r_s.append(hr_s.reshape(bs, S5_GROUPS, S5_STATE))
            s5i_s.append(hi_s.reshape(bs, S5_GROUPS, S5_STATE))
            y = jnp.concatenate([y_p, y_s], 0)
            s5_out = gmm(y, (s5_w_glu[e],), tm=ROW, tn=512, out_dtype=BF16, epilogue="glu", extras=(y,),
                         name="s5_glu")

            cqn, ckv = mla_norms(h, mla_q_norm[e], mla_kv_norm[e])
            kr = jnp.pad(h[:, S5_WIDTH + MLA_Q_RANK + MLA_KV_RANK:], ((0, 0), (0, LANES - MLA_ROPE)))
            kr_rot = jnp.concatenate([-kr[:, MLA_ROPE // 2:MLA_ROPE], kr[:, :MLA_ROPE // 2],
                                      kr[:, MLA_ROPE:]], 1)
            k_pe = rope_pair(kr, 0, kr_rot, 0, LANES, cos_mla, sin_mla, out_dtype=F32)
            lat_p.append(ckv[:n_p].reshape(bp, tp, MLA_KV_RANK))
            lat_s.append(ckv[n_p:].reshape(bs, ts, MLA_KV_RANK))
            pe_p.append(k_pe[:n_p, :MLA_ROPE].reshape(bp, tp, MLA_ROPE))
            pe_s.append(k_pe[n_p:, :MLA_ROPE].reshape(bs, ts, MLA_ROPE))

            w_uq = mla_w_uq[e]
            w_nope = w_uq[:, :, :MLA_NOPE].reshape(MLA_Q_RANK, MLA_HEADS * MLA_NOPE)
            w_pe = w_uq[:, :, MLA_NOPE:]
            pad = ((0, 0), (0, 0), (0, LANES - MLA_ROPE))
            w_pe_pad = jnp.pad(w_pe, pad).reshape(MLA_Q_RANK, MLA_HEADS * LANES)
            w_rot_pad = jnp.pad(_rot_half_cols(w_pe.reshape(MLA_Q_RANK, -1), MLA_ROPE)
                                .reshape(MLA_Q_RANK, MLA_HEADS, MLA_ROPE), pad).reshape(MLA_Q_RANK, -1)
            w_q_ext = jnp.concatenate([w_nope, w_pe_pad, w_rot_pad], 1)
            q_ext = _dense(cqn, w_q_ext, tn=512, out_dtype=F32, name="mla_q")
            hw = MLA_HEADS * LANES
            q_pe = rope_pair(q_ext, 1, q_ext, 2, hw, cos_mla, sin_mla, out_dtype=BF16)

            ckv_b = ckv.astype(BF16)
            w_uk2 = mla_w_uk[e].reshape(MLA_KV_RANK, MLA_HEADS * MLA_NOPE)
            w_uv2 = mla_w_uv[e].reshape(MLA_KV_RANK, MLA_HEADS * MLA_V)
            k_nope = _dense(ckv_b, w_uk2, tn=512, out_dtype=BF16, n_rows=n_p, name="mla_k_up")
            v_full = _dense(ckv_b, w_uv2, tn=512, out_dtype=BF16, n_rows=n_p, name="mla_v_up")
            o_p = mla_prompt_attention(q_ext, q_pe, k_nope, k_pe, v_full, bp, tp, scale=MLA_SCALE)

            q_lat = per_head_matmul(q_ext, n_p // n_s, n_s, MLA_NOPE, w_uk2, MLA_KV_RANK,
                                    transpose_w=True, out_dtype=BF16)
            q_lat = q_lat.reshape(bs, ts * MLA_HEADS, MLA_KV_RANK)
            q_pe_s = q_pe[n_p:].reshape(bs, ts * MLA_HEADS, LANES)[:, :, :MLA_ROPE]
            o_lat = mla_sample_attention(page_table, q_lat, q_pe_s, lat_s[-1], pe_s[-1],
                                         cache_mla_lat, cache_mla_pe, e, scale=MLA_SCALE)
            o_lat = o_lat.reshape(n_s, MLA_HEADS * MLA_KV_RANK)
            o_s = per_head_matmul(o_lat, 0, n_s, MLA_KV_RANK, w_uv2, MLA_V, transpose_w=False, out_dtype=BF16)
            mla_out = jnp.concatenate([o_p, o_s], 0)
            mix = _dense(jnp.concatenate([s5_out, mla_out], 1), w_out_even[e], tn=512, out_dtype=F32,
                         name="even_out")
        else:
            o = l // 2
            h = _dense(xb, w_in_odd[o], tn=512, out_dtype=F32, name="odd_in")
            qk = rope_roll(h, C_Q_WIDTH + C_KV_WIDTH, cos_c, sin_c)
            k_p = qk[:n_p, C_Q_WIDTH:]
            v_p = h[:n_p, C_Q_WIDTH + C_KV_WIDTH:]
            keep = min(C_PATTERNS[-1][0], tp)
            ck_p.append(k_p.reshape(bp, tp, C_KV_HEADS, C_HEAD_DIM)[:, tp - keep:])
            cv_p.append(v_p.reshape(bp, tp, C_KV_HEADS, C_HEAD_DIM)[:, tp - keep:])
            outs, lses = [], []
            for g, (_, dil) in enumerate(C_PATTERNS):
                def classes(a, width):
                    a = a.reshape(bp, tp // dil, dil, width)
                    return jnp.transpose(a, (0, 2, 1, 3)).reshape(bp * dil, tp // dil, width)
                qg = classes(qk[:n_p, g * C_OUT_WIDTH:(g + 1) * C_OUT_WIDTH], C_OUT_WIDTH)
                og, lg = window_attention(qg, classes(k_p, C_KV_WIDTH), classes(v_p, C_KV_WIDTH))
                def unclasses(a):
                    a = a.reshape(bp, dil, tp // dil, C_OUT_WIDTH)
                    return jnp.transpose(a, (0, 2, 1, 3)).reshape(n_p, C_OUT_WIDTH)
                outs.append(unclasses(og))
                lses.append(unclasses(lg))
            o_p = merge_groups(outs, lses)
            o_s, nk, nv = dilated_sample(qk, h, n_p // ts, bs, ts, cache_ck, cache_cv, o)
            ck_s.append(nk.reshape(bs, c_buf, C_KV_HEADS, C_HEAD_DIM))
            cv_s.append(nv.reshape(bs, c_buf, C_KV_HEADS, C_HEAD_DIM))
            attn = jnp.concatenate([o_p, o_s.astype(BF16)], 0)
            mix = _dense(attn, w_out_odd[o], tn=512, out_dtype=F32, name="odd_out")

        x, xb = residual_ln(x, mix, ln_g[l, 0], ln_b[l, 0])

        mk = _dense(mem_b, mem_w_k[l], tn=512, out_dtype=F32, name="mem_k")
        mv = _dense(mem_b, mem_w_v[l], tn=512, out_dtype=F32, name="mem_v")
        mk_p.append(mk.reshape(bp, mem_len, MEM_HEADS, MEM_HEAD_DIM))
        mv_p.append(mv.reshape(bp, mem_len, MEM_HEADS, MEM_HEAD_DIM))
        q_mem = _dense(xb, mem_w_q[l], tn=512, out_dtype=F32, name="mem_q")
        kv3 = (None, mem_len, MEM_WIDTH)
        ca_p = cross_attention(q_mem, 0, bp, tp, mk.reshape(bp, mem_len, MEM_WIDTH),
                               mv.reshape(bp, mem_len, MEM_WIDTH), lambda b, i: (b, 0, 0), kv3, tq=512)
        ca_s = cross_attention(q_mem, n_p // ts, bs, ts, cache_mk, cache_mv,
                               lambda b, i, l=l: (l, b, 0, 0), (None,) + kv3, tq=ts)
        ca = jnp.concatenate([ca_p, ca_s], 0)
        f = _dense(ca, mem_w_o[l], tn=512, out_dtype=F32, name="mem_o")
        x, xb = residual_ln(x, f, ln_g[l, 1], ln_b[l, 1])

        if l % 2 == 0:
            e = l // 2
            hid = gmm(xb, (ffn_w_gate[e], ffn_w_up[e]), tm=ROW, tn=256, out_dtype=BF16, epilogue="swiglu",
                      name="ffn_up")
            f = mm(hid, ffn_w_down[e], tm=ROW, tn=1024, tk=1024, out_dtype=F32, name="ffn_down")
            x, xb = residual_ln(x, f, ln_g[l, 2], ln_b[l, 2])
        else:
            o = l // 2
            x, xb = moe_block(x, xb, moe_w_router[o], moe_w_gate[o], moe_w_up[o], moe_w_down[o],
                              ln_g[l, 2], ln_b[l, 2])

    return (x[:n_p].reshape(bp, tp, d), x[n_p:].reshape(bs, ts, d),
            jnp.stack(lat_p), jnp.stack(pe_p), jnp.stack(lat_s), jnp.stack(pe_s),
            jnp.stack(s5r_p), jnp.stack(s5i_p), jnp.stack(s5r_s), jnp.stack(s5i_s),
            jnp.stack(ck_p), jnp.stack(cv_p), jnp.stack(ck_s), jnp.stack(cv_s),
            jnp.stack(mk_p), jnp.stack(mv_p))
```

```python
import functools
import math

import jax
import jax.numpy as jnp
from jax import lax
from jax.experimental import pallas as pl
from jax.experimental.pallas import tpu as pltpu

F32 = jnp.float32
BF16 = jnp.bfloat16
I32 = jnp.int32

VMEM_LIMIT_BYTES = 56 * 1024 * 1024
LANES = 128
SUBLANES = 8

D_MODEL = 4096
DEPTH = 4
PAGE_SIZE = 128
S5_WIDTH = D_MODEL // 2
S5_GROUP = 16
S5_GROUPS = S5_WIDTH // S5_GROUP
S5_STATE = 64
S5_FLAT = S5_GROUPS * S5_STATE
S5_SUPER = 16
S5_NSUPER = S5_GROUPS // S5_SUPER
MLA_HEADS = 16
MLA_NOPE = 128
MLA_ROPE = 64
MLA_V = 128
MLA_KV_RANK = 512
MLA_Q_RANK = D_MODEL // 4
MLA_WIDTH = MLA_HEADS * MLA_V
MLA_SCALE = (MLA_NOPE + MLA_ROPE) ** -0.5
C_HEAD_DIM = 128
C_PATTERNS = ((128, 1), (512, 4), (2048, 16))
C_GROUPS = len(C_PATTERNS)
C_SLOTS = 8
C_KV_HEADS = 4
C_REP = C_SLOTS // C_KV_HEADS
C_SPAN = 128
C_Q_WIDTH = C_GROUPS * C_SLOTS * C_HEAD_DIM
C_KV_WIDTH = C_KV_HEADS * C_HEAD_DIM
C_OUT_WIDTH = C_SLOTS * C_HEAD_DIM
C_SCALE = C_HEAD_DIM ** -0.5
MEM_HEADS = 4
MEM_HEAD_DIM = 128
MEM_WIDTH = MEM_HEADS * MEM_HEAD_DIM
MEM_SCALE = MEM_HEAD_DIM ** -0.5
N_EXPERTS = 8
TOP_K = 2
ROPE_THETA = 10000.0
LN_EPS = 1e-5
RMS_EPS = 1e-6
DEEPNORM_ALPHA = (2.0 * DEPTH) ** 0.25
NEG_INF = float("-inf")

assert all(w // d == C_SPAN for w, d in C_PATTERNS)


def _params(*sem):
    return pltpu.CompilerParams(dimension_semantics=sem, vmem_limit_bytes=VMEM_LIMIT_BYTES)


def _dot(a, b):
    return jnp.dot(a, b, preferred_element_type=F32)


def _dot_nt(a, b):
    return lax.dot_general(a, b, (((1,), (1,)), ((), ())), preferred_element_type=F32)


def _gmm_kernel(tg_ref, x_ref, *refs, n_w, epilogue, n_extra, row_chunk):
    w_refs = refs[:n_w]
    extra_refs = refs[n_w:n_w + n_extra]
    o_ref = refs[n_w + n_extra]
    wb_refs = refs[n_w + n_extra + 1:]
    i = pl.program_id(1)
    prev = tg_ref[jnp.maximum(i - 1, 0)]
    changed = jnp.logical_or(i == 0, tg_ref[i] != prev)

    @pl.when(changed)
    def _():
        depth = w_refs[0].shape[0]
        step = 512 if depth % 512 == 0 else depth
        for w_ref, wb_ref in zip(w_refs, wb_refs):
            for r in range(0, depth, step):
                wb_ref[r:r + step, :] = w_ref[r:r + step, :].astype(BF16)

    tm = x_ref.shape[0]
    for r in range(0, tm, row_chunk):
        x = x_ref[r:r + row_chunk, :].astype(BF16)
        acc = _dot(x, wb_refs[0][...])
        if epilogue == "swiglu":
            up = _dot(x, wb_refs[1][...])
            acc = acc * jax.nn.sigmoid(acc) * up
        elif epilogue == "glu":
            acc = extra_refs[0][r:r + row_chunk, :] * jax.nn.sigmoid(acc)
        o_ref[r:r + row_chunk, :] = acc.astype(o_ref.dtype)


def gmm(x, ws, tile_group=None, *, tm, tn, out_dtype, epilogue="none", extras=(), x_col_block=0,
        row_block_offset=0, n_rows=None, group=0, name="gmm"):
    ws = tuple(w.reshape((-1,) + w.shape[-2:]) for w in ws)
    depth, n_out = ws[0].shape[1], ws[0].shape[2]
    rows = x.shape[0] if n_rows is None else n_rows
    assert rows % tm == 0
    ni, nj = rows // tm, pl.cdiv(n_out, tn)
    if tile_group is None:
        tile_group = jnp.full((ni,), group, I32)
    row_chunk = min(tm, 256)
    kern = functools.partial(_gmm_kernel, n_w=len(ws), epilogue=epilogue, n_extra=len(extras),
                             row_chunk=row_chunk)
    in_specs = [pl.BlockSpec((tm, depth), lambda j, i, tg: (i + row_block_offset, x_col_block))]
    in_specs += [pl.BlockSpec((None, depth, tn), lambda j, i, tg: (tg[i], 0, j)) for _ in ws]
    in_specs += [pl.BlockSpec((tm, tn), lambda j, i, tg: (i + row_block_offset, j)) for _ in extras]
    return pl.pallas_call(
        kern,
        out_shape=jax.ShapeDtypeStruct((rows, n_out), out_dtype),
        grid_spec=pltpu.PrefetchScalarGridSpec(
            num_scalar_prefetch=1, grid=(nj, ni), in_specs=in_specs,
            out_specs=pl.BlockSpec((tm, tn), lambda j, i, tg: (i, j)),
            scratch_shapes=[pltpu.VMEM((depth, tn), BF16) for _ in ws]),
        compiler_params=_params("arbitrary", "arbitrary"),
        name=name,
    )(tile_group, x, *ws, *extras)


def _mm_kernel(x_ref, w_ref, o_ref, acc_ref, *, depth, tk):
    k = pl.program_id(2)
    nk = pl.num_programs(2)

    @pl.when(k == 0)
    def _():
        acc_ref[...] = jnp.zeros_like(acc_ref)

    rem = depth % tk

    def full():
        acc_ref[...] += _dot(x_ref[...].astype(BF16), w_ref[...].astype(BF16))

    if rem == 0:
        full()
    else:
        pl.when(k < nk - 1)(full)

        @pl.when(k == nk - 1)
        def _():
            acc_ref[...] += _dot(x_ref[:, :rem].astype(BF16), w_ref[:rem, :].astype(BF16))

    @pl.when(k == nk - 1)
    def _():
        o_ref[...] = acc_ref[...].astype(o_ref.dtype)


def mm(x, w, layer, *, tm, tn, tk, out_dtype, name="mm"):
    rows, depth = x.shape
    n_out = w.shape[2]
    assert rows % tm == 0 and n_out % tn == 0 and (depth % tk) % LANES == 0
    grid = (rows // tm, n_out // tn, pl.cdiv(depth, tk))
    return pl.pallas_call(
        functools.partial(_mm_kernel, depth=depth, tk=tk),
        out_shape=jax.ShapeDtypeStruct((rows, n_out), out_dtype),
        grid=grid,
        in_specs=[pl.BlockSpec((tm, tk), lambda i, j, k: (i, k)),
                  pl.BlockSpec((None, tk, tn), lambda i, j, k: (layer, k, j))],
        out_specs=pl.BlockSpec((tm, tn), lambda i, j, k: (i, j)),
        scratch_shapes=[pltpu.VMEM((tm, tn), F32)],
        compiler_params=_params("arbitrary", "arbitrary", "arbitrary"),
        name=name,
    )(x, w)


def _layer_norm_rows(v, g, b):
    mu = jnp.mean(v, -1, keepdims=True)
    c = v - mu
    var = jnp.mean(c * c, -1, keepdims=True)
    return c * lax.rsqrt(var + LN_EPS) * g + b


def _ln_kernel(x_ref, f_ref, g_ref, b_ref, o_ref, ob_ref):
    v = DEEPNORM_ALPHA * x_ref[...] + f_ref[...].astype(F32)
    y = _layer_norm_rows(v, g_ref[...], b_ref[...])
    o_ref[...] = y
    ob_ref[...] = y.astype(BF16)


def residual_ln(x, f, g, b, *, tm=256):
    rows, width = x.shape
    row = pl.BlockSpec((tm, width), lambda i: (i, 0))
    vec = pl.BlockSpec((1, width), lambda i: (0, 0))
    return pl.pallas_call(
        _ln_kernel,
        out_shape=(jax.ShapeDtypeStruct((rows, width), F32), jax.ShapeDtypeStruct((rows, width), BF16)),
        grid=(rows // tm,),
        in_specs=[row, row, vec, vec],
        out_specs=(row, row),
        compiler_params=_params("arbitrary"),
        name="residual_ln",
    )(x, f, g.reshape(1, width), b.reshape(1, width))


def _rms(v, g):
    return v * lax.rsqrt(jnp.mean(v * v, -1, keepdims=True) + RMS_EPS) * g


def _mla_norm_kernel(cq_ref, ckv_ref, gq_ref, gkv_ref, oq_ref, okv_ref):
    oq_ref[...] = _rms(cq_ref[...], gq_ref[...]).astype(BF16)
    okv_ref[...] = _rms(ckv_ref[...], gkv_ref[...])


def mla_norms(h, q_norm, kv_norm, *, tm=512):
    rows = h.shape[0]
    q_blk = S5_WIDTH // MLA_Q_RANK
    kv_blk = (S5_WIDTH + MLA_Q_RANK) // MLA_KV_RANK
    return pl.pallas_call(
        _mla_norm_kernel,
        out_shape=(jax.ShapeDtypeStruct((rows, MLA_Q_RANK), BF16),
                   jax.ShapeDtypeStruct((rows, MLA_KV_RANK), F32)),
        grid=(rows // tm,),
        in_specs=[pl.BlockSpec((tm, MLA_Q_RANK), lambda i: (i, q_blk)),
                  pl.BlockSpec((tm, MLA_KV_RANK), lambda i: (i, kv_blk)),
                  pl.BlockSpec((1, MLA_Q_RANK), lambda i: (0, 0)),
                  pl.BlockSpec((1, MLA_KV_RANK), lambda i: (0, 0))],
        out_specs=(pl.BlockSpec((tm, MLA_Q_RANK), lambda i: (i, 0)),
                   pl.BlockSpec((tm, MLA_KV_RANK), lambda i: (i, 0))),
        compiler_params=_params("arbitrary"),
        name="mla_norms",
    )(h, h, q_norm.reshape(1, -1), kv_norm.reshape(1, -1))


def _rope_pair_kernel(x_ref, xr_ref, cos_ref, sin_ref, o_ref, *, hd):
    c, s = cos_ref[...], sin_ref[...]
    for h in range(x_ref.shape[1] // hd):
        sl = slice(h * hd, (h + 1) * hd)
        o_ref[:, sl] = (x_ref[:, sl] * c + xr_ref[:, sl] * s).astype(o_ref.dtype)


def rope_pair(src_x, x_blk, src_r, r_blk, width, cos, sin, *, out_dtype, tm=512):
    rows = src_x.shape[0]
    hd = cos.shape[1]
    return pl.pallas_call(
        functools.partial(_rope_pair_kernel, hd=hd),
        out_shape=jax.ShapeDtypeStruct((rows, width), out_dtype),
        grid=(rows // tm,),
        in_specs=[pl.BlockSpec((tm, width), lambda i: (i, x_blk)),
                  pl.BlockSpec((tm, width), lambda i: (i, r_blk)),
                  pl.BlockSpec((tm, hd), lambda i: (i, 0)),
                  pl.BlockSpec((tm, hd), lambda i: (i, 0))],
        out_specs=pl.BlockSpec((tm, width), lambda i: (i, 0)),
        compiler_params=_params("arbitrary"),
        name="rope_pair",
    )(src_x, src_r, cos, sin)


def _rope_roll_kernel(x_ref, cos_ref, sin_ref, o_ref):
    c, s = cos_ref[...], sin_ref[...]
    for h in range(x_ref.shape[1] // LANES):
        sl = slice(h * LANES, (h + 1) * LANES)
        x = x_ref[:, sl]
        o_ref[:, sl] = x * c + pltpu.roll(x, LANES // 2, 1) * s


def rope_roll(src, width, cos, sin_signed, *, tm=256):
    rows = src.shape[0]
    return pl.pallas_call(
        _rope_roll_kernel,
        out_shape=jax.ShapeDtypeStruct((rows, width), F32),
        grid=(rows // tm,),
        in_specs=[pl.BlockSpec((tm, width), lambda i: (i, 0)),
                  pl.BlockSpec((tm, LANES), lambda i: (i, 0)),
                  pl.BlockSpec((tm, LANES), lambda i: (i, 0))],
        out_specs=pl.BlockSpec((tm, width), lambda i: (i, 0)),
        compiler_params=_params("arbitrary"),
        name="rope_roll",
    )(src, cos, sin_signed)


def _s5_param_kernel(lr_ref, li_ref, ldt_ref, bre_ref, bim_ref, pre_ref, pim_ref, bbre_ref, bbim_ref):
    lr, li = lr_ref[...], li_ref[...]
    dt = jnp.exp(ldt_ref[...])
    for k in range(SUBLANES):
        decay = jnp.exp(lr * dt * (k + 1.0))
        pre_ref[k] = decay * jnp.cos(li * dt * (k + 1.0))
        pim_ref[k] = decay * jnp.sin(li * dt * (k + 1.0))
    a_re, a_im = pre_ref[0], pim_ref[0]
    inv_den = 1.0 / (lr * lr + li * li)
    f_re = ((a_re - 1.0) * lr + a_im * li) * inv_den
    f_im = (a_im * lr - (a_re - 1.0) * li) * inv_den
    for c in range(S5_GROUP):
        br, bi = bre_ref[c], bim_ref[c]
        bbre_ref[c] = f_re * br - f_im * bi
        bbim_ref[c] = f_re * bi + f_im * br


def s5_params(lam_re, lam_im, log_dt, b_re, b_im):
    g, p = lam_re.shape
    b_re_t = jnp.transpose(b_re, (2, 0, 1))
    b_im_t = jnp.transpose(b_im, (2, 0, 1))
    return pl.pallas_call(
        _s5_param_kernel,
        out_shape=(jax.ShapeDtypeStruct((SUBLANES, g, p), F32), jax.ShapeDtypeStruct((SUBLANES, g, p), F32),
                   jax.ShapeDtypeStruct((S5_GROUP, g, p), F32), jax.ShapeDtypeStruct((S5_GROUP, g, p), F32)),
        name="s5_params",
    )(lam_re, lam_im, log_dt.reshape(g, 1), b_re_t, b_im_t)


def _block_diag(blocks):
    s, n, r, c = blocks.shape
    eye = jnp.eye(n, dtype=blocks.dtype)
    return (blocks[:, :, :, None, :] * eye[None, :, None, :, None]).reshape(s, n * r, n * c)


def _gelu_tanh(y):
    return 0.5 * y * (1.0 + jnp.tanh(math.sqrt(2.0 / math.pi) * (y + 0.044715 * (y * y * y))))


S5_LANE_CHUNK = 512


def _s5_kernel(u_ref, h0re_ref, h0im_ref, pre_ref, pim_ref, wb_ref, wcre_ref, wcim_ref, d_ref,
               y_ref, hre_ref, him_ref, sre_ref, sim_ref, cre_ref, cim_ref, *, reset):
    rows = u_ref.shape[0]
    n_groups = rows // SUBLANES
    sup_u = S5_SUPER * S5_GROUP
    sup_h = S5_SUPER * S5_STATE

    for s in range(S5_NSUPER):
        ub = u_ref[:, s * sup_u:(s + 1) * sup_u].astype(BF16)
        bu = _dot(ub, wb_ref[s])
        sre_ref[:, s * sup_h:(s + 1) * sup_h] = bu[:, :sup_h]
        sim_ref[:, s * sup_h:(s + 1) * sup_h] = bu[:, sup_h:]

    if not reset:
        @pl.when(pl.program_id(1) == 0)
        def _():
            cre_ref[...] = h0re_ref[...]
            cim_ref[...] = h0im_ref[...]

    row_id = lax.broadcasted_iota(I32, (SUBLANES, S5_LANE_CHUNK), 0)
    for c0 in range(0, S5_FLAT, S5_LANE_CHUNK):
        lanes = slice(c0, c0 + S5_LANE_CHUNK)
        p_re, p_im = pre_ref[:, lanes], pim_ref[:, lanes]

        def body(gi, carry):
            car_re, car_im = carry
            r0 = pl.multiple_of(gi * SUBLANES, SUBLANES)
            x_re = sre_ref[pl.ds(r0, SUBLANES), lanes]
            x_im = sim_ref[pl.ds(r0, SUBLANES), lanes]
            for lvl, d in enumerate((1, 2, 4)):
                a_re = p_re[d - 1:d, :]
                a_im = p_im[d - 1:d, :]
                keep = row_id >= d
                s_re = jnp.where(keep, pltpu.roll(x_re, d, 0), 0.0)
                s_im = jnp.where(keep, pltpu.roll(x_im, d, 0), 0.0)
                x_re, x_im = (x_re + a_re * s_re - a_im * s_im,
                              x_im + a_re * s_im + a_im * s_re)
            if reset:
                car_re = h0re_ref[pl.ds(gi, 1), lanes]
                car_im = h0im_ref[pl.ds(gi, 1), lanes]
            h_re = x_re + p_re * car_re - p_im * car_im
            h_im = x_im + p_re * car_im + p_im * car_re
            sre_ref[pl.ds(r0, SUBLANES), lanes] = h_re
            sim_ref[pl.ds(r0, SUBLANES), lanes] = h_im
            last_re = h_re[SUBLANES - 1:SUBLANES, :]
            last_im = h_im[SUBLANES - 1:SUBLANES, :]
            if reset:
                hre_ref[pl.ds(gi, 1), lanes] = last_re
                him_ref[pl.ds(gi, 1), lanes] = last_im
            return last_re, last_im

        if reset:
            init = (jnp.zeros((1, S5_LANE_CHUNK), F32), jnp.zeros((1, S5_LANE_CHUNK), F32))
        else:
            init = (cre_ref[:, lanes], cim_ref[:, lanes])
        fin_re, fin_im = lax.fori_loop(0, n_groups, body, init)
        if not reset:
            cre_ref[:, lanes] = fin_re
            cim_ref[:, lanes] = fin_im
            hre_ref[:, lanes] = fin_re
            him_ref[:, lanes] = fin_im

    for s in range(S5_NSUPER):
        h_re = sre_ref[:, s * sup_h:(s + 1) * sup_h].astype(BF16)
        h_im = sim_ref[:, s * sup_h:(s + 1) * sup_h].astype(BF16)
        cols = slice(s * sup_u, (s + 1) * sup_u)
        y = _dot(h_re, wcre_ref[s]) + _dot(h_im, wcim_ref[s]) + d_ref[:, cols] * u_ref[:, cols]
        y_ref[:, cols] = _gelu_tanh(y)


def s5_core(h_in, row_block_offset, n_seq, seq_len, h0_re, h0_im, consts, *, reset, tile_rows):
    pow_re, pow_im, wb, wc_re, wc_im, d_skip = consts
    rows = n_seq * seq_len
    const2 = lambda *_: (0, 0)
    const3 = lambda *_: (0, 0, 0)
    w_specs = [pl.BlockSpec(pow_re.shape, const2), pl.BlockSpec(pow_im.shape, const2),
               pl.BlockSpec(wb.shape, const3), pl.BlockSpec(wc_re.shape, const3),
               pl.BlockSpec(wc_im.shape, const3), pl.BlockSpec(d_skip.shape, const2)]
    scratch = [pltpu.VMEM((tile_rows, S5_FLAT), F32), pltpu.VMEM((tile_rows, S5_FLAT), F32),
               pltpu.VMEM((1, S5_FLAT), F32), pltpu.VMEM((1, S5_FLAT), F32)]
    kern = functools.partial(_s5_kernel, reset=reset)
    if reset:
        assert seq_len == SUBLANES and rows % tile_rows == 0
        seqs = tile_rows // SUBLANES
        grid = (rows // tile_rows,)
        u_spec = pl.BlockSpec((tile_rows, S5_WIDTH), lambda i: (i + row_block_offset, 0))
        h_spec = pl.BlockSpec((seqs, S5_FLAT), lambda i: (i, 0))
        y_spec = pl.BlockSpec((tile_rows, S5_WIDTH), lambda i: (i, 0))
        st_shape = jax.ShapeDtypeStruct((n_seq, S5_FLAT), F32)
        sem = ("arbitrary",)
    else:
        assert seq_len % tile_rows == 0
        per = seq_len // tile_rows
        grid = (n_seq, per)
        u_spec = pl.BlockSpec((tile_rows, S5_WIDTH), lambda b, i: (b * per + i + row_block_offset, 0))
        h_spec = pl.BlockSpec((None, 1, S5_FLAT), lambda b, i: (b, 0, 0))
        y_spec = pl.BlockSpec((tile_rows, S5_WIDTH), lambda b, i: (b * per + i, 0))
        st_shape = jax.ShapeDtypeStruct((n_seq, 1, S5_FLAT), F32)
        h0_re = h0_re.reshape(n_seq, 1, S5_FLAT)
        h0_im = h0_im.reshape(n_seq, 1, S5_FLAT)
        sem = ("arbitrary", "arbitrary")
    y, st_re, st_im = pl.pallas_call(
        kern,
        out_shape=(jax.ShapeDtypeStruct((rows, S5_WIDTH), F32), st_shape, st_shape),
        grid=grid,
        in_specs=[u_spec, h_spec, h_spec] + w_specs,
        out_specs=(y_spec, h_spec, h_spec),
        scratch_shapes=scratch,
        compiler_params=_params(*sem),
        name="s5_reset" if reset else "s5_scan",
    )(h_in, h0_re, h0_im, pow_re, pow_im, wb, wc_re, wc_im, d_skip)
    return y, st_re.reshape(n_seq, S5_FLAT), st_im.reshape(n_seq, S5_FLAT)


def s5_constants(lam_re, lam_im, log_dt, b_re, b_im, c_re, c_im, d_skip):
    pow_re, pow_im, bb_re, bb_im = s5_params(lam_re, lam_im, log_dt, b_re, b_im)
    pow_re = pow_re.reshape(SUBLANES, S5_FLAT)
    pow_im = pow_im.reshape(SUBLANES, S5_FLAT)
    def to_blocks(bb):
        return jnp.transpose(bb, (1, 0, 2)).reshape(S5_NSUPER, S5_SUPER, S5_GROUP, S5_STATE)
    wb = jnp.concatenate([_block_diag(to_blocks(bb_re)), _block_diag(to_blocks(bb_im))], -1).astype(BF16)
    def c_blocks(c):
        return jnp.transpose(c, (0, 2, 1)).reshape(S5_NSUPER, S5_SUPER, S5_STATE, S5_GROUP)
    wc_re = _block_diag(c_blocks(c_re)).astype(BF16)
    wc_im = _block_diag(c_blocks(-c_im)).astype(BF16)
    return pow_re, pow_im, wb, wc_re, wc_im, d_skip.reshape(1, S5_WIDTH)


def _flash_kernel(qn_ref, qp_ref, kn_ref, kp_ref, v_ref, o_ref, m_ref, l_ref, acc_ref, *, scale):
    i, j = pl.program_id(2), pl.program_id(3)
    tq, tk = qn_ref.shape[0], kn_ref.shape[0]

    @pl.when(j == 0)
    def _():
        m_ref[...] = jnp.full_like(m_ref, NEG_INF)
        l_ref[...] = jnp.zeros_like(l_ref)
        acc_ref[...] = jnp.zeros_like(acc_ref)

    def step(diagonal):
        q = jnp.concatenate([qn_ref[...].astype(BF16), qp_ref[...].astype(BF16)], -1)
        k = jnp.concatenate([kn_ref[...].astype(BF16), kp_ref[...].astype(BF16)], -1)
        s = _dot_nt(q, k) * scale
        if diagonal:
            qpos = lax.broadcasted_iota(I32, (tq, tk), 0)
            kpos = lax.broadcasted_iota(I32, (tq, tk), 1)
            s = jnp.where(kpos <= qpos, s, NEG_INF)
        m_new = jnp.maximum(m_ref[...], jnp.max(s, -1, keepdims=True))
        corr = jnp.exp(m_ref[...] - m_new)
        p = jnp.exp(s - m_new)
        l_ref[...] = corr * l_ref[...] + jnp.sum(p, -1, keepdims=True)
        acc_ref[...] = corr * acc_ref[...] + _dot(p.astype(BF16), v_ref[...].astype(BF16))
        m_ref[...] = m_new

    pl.when(j < i)(functools.partial(step, False))
    pl.when(j == i)(functools.partial(step, True))

    @pl.when(j == pl.num_programs(3) - 1)
    def _():
        o_ref[...] = (acc_ref[...] / l_ref[...]).astype(o_ref.dtype)


def mla_prompt_attention(q_ext, q_pe, k_nope, k_pe, v, n_seq, seq_len, *, scale, tq=512, tk=512):
    assert tq == tk
    nq, nk = seq_len // tq, seq_len // tk
    hd = LANES

    def qmap(b, h, i, j):
        return (b * nq + i, h)

    def kmap(b, h, i, j):
        return (b * nk + jnp.minimum(j, i), h)

    def kpmap(b, h, i, j):
        return (b * nk + jnp.minimum(j, i), 0)

    return pl.pallas_call(
        functools.partial(_flash_kernel, scale=scale),
        out_shape=jax.ShapeDtypeStruct((n_seq * seq_len, MLA_HEADS * hd), BF16),
        grid=(n_seq, MLA_HEADS, nq, nk),
        in_specs=[pl.BlockSpec((tq, hd), qmap), pl.BlockSpec((tq, hd), qmap),
                  pl.BlockSpec((tk, hd), kmap), pl.BlockSpec((tk, hd), kpmap),
                  pl.BlockSpec((tk, hd), kmap)],
        out_specs=pl.BlockSpec((tq, hd), qmap),
        scratch_shapes=[pltpu.VMEM((tq, 1), F32), pltpu.VMEM((tq, 1), F32), pltpu.VMEM((tq, hd), F32)],
        compiler_params=_params("arbitrary", "arbitrary", "arbitrary", "arbitrary"),
        name="mla_prompt_attention",
    )(q_ext, q_pe, k_nope, k_pe, v)


PAGES_PER_STEP = 8


def _paged_kernel(pt_ref, ql_ref, qp_ref, cn_ref, pn_ref, *refs, scale, heads):
    lat_refs = refs[:PAGES_PER_STEP]
    pe_refs = refs[PAGES_PER_STEP:2 * PAGES_PER_STEP]
    o_ref, m_ref, l_ref, acc_ref = refs[2 * PAGES_PER_STEP:]
    j = pl.program_id(1)
    rows = ql_ref.shape[0]

    @pl.when(j == 0)
    def _():
        m_ref[...] = jnp.full_like(m_ref, NEG_INF)
        l_ref[...] = jnp.zeros_like(l_ref)
        acc_ref[...] = jnp.zeros_like(acc_ref)

    ql, qp = ql_ref[...], qp_ref[...]
    lats = [r[...].astype(BF16) for r in lat_refs]
    s = jnp.concatenate(
        [_dot_nt(ql, lat) + _dot(qp, pe[...].astype(BF16)) for lat, pe in zip(lats, pe_refs)], -1) * scale
    m_new = jnp.maximum(m_ref[...], jnp.max(s, -1, keepdims=True))
    corr = jnp.exp(m_ref[...] - m_new)
    p = jnp.exp(s - m_new)
    l_ref[...] = corr * l_ref[...] + jnp.sum(p, -1, keepdims=True)
    pv = _dot(p[:, :PAGE_SIZE].astype(BF16), lats[0])
    for n in range(1, PAGES_PER_STEP):
        pv += _dot(p[:, n * PAGE_SIZE:(n + 1) * PAGE_SIZE].astype(BF16), lats[n])
    acc_ref[...] = corr * acc_ref[...] + pv
    m_ref[...] = m_new

    @pl.when(j == pl.num_programs(1) - 1)
    def _():
        cn = cn_ref[...].astype(BF16)
        t_new = cn.shape[0]
        sn = (_dot_nt(ql, cn) + _dot_nt(qp, pn_ref[...].astype(BF16))) * scale
        q_tok = lax.broadcasted_iota(I32, (rows, t_new), 0) // heads
        k_tok = lax.broadcasted_iota(I32, (rows, t_new), 1)
        sn = jnp.where(k_tok <= q_tok, sn, NEG_INF)
        m_fin = jnp.maximum(m_ref[...], jnp.max(sn, -1, keepdims=True))
        corr2 = jnp.exp(m_ref[...] - m_fin)
        pn = jnp.exp(sn - m_fin)
        l_fin = corr2 * l_ref[...] + jnp.sum(pn, -1, keepdims=True)
        acc = corr2 * acc_ref[...] + _dot(pn.astype(BF16), cn)
        o_ref[...] = (acc / l_fin).astype(o_ref.dtype)


def mla_sample_attention(page_table, q_lat, q_pe, c_new, pe_new, cache_lat, cache_pe_t, layer, *, scale):
    n_seq, rows, rank = q_lat.shape
    n_pages = page_table.shape[1]
    assert n_pages % PAGES_PER_STEP == 0
    steps = n_pages // PAGES_PER_STEP
    t_new = c_new.shape[1]
    rope = q_pe.shape[2]

    def seq_map(b, j, pt):
        return (b, 0, 0)

    def page_map(n):
        return lambda b, j, pt: (layer, pt[b * n_pages + j * PAGES_PER_STEP + n], 0, 0)

    in_specs = [pl.BlockSpec((None, rows, rank), seq_map), pl.BlockSpec((None, rows, rope), seq_map),
                pl.BlockSpec((None, t_new, rank), seq_map), pl.BlockSpec((None, t_new, rope), seq_map)]
    in_specs += [pl.BlockSpec((None, None, PAGE_SIZE, rank), page_map(n)) for n in range(PAGES_PER_STEP)]
    in_specs += [pl.BlockSpec((None, None, rope, PAGE_SIZE), page_map(n)) for n in range(PAGES_PER_STEP)]
    return pl.pallas_call(
        functools.partial(_paged_kernel, scale=scale, heads=rows // t_new),
        out_shape=jax.ShapeDtypeStruct((n_seq, rows, rank), BF16),
        grid_spec=pltpu.PrefetchScalarGridSpec(
            num_scalar_prefetch=1, grid=(n_seq, steps), in_specs=in_specs,
            out_specs=pl.BlockSpec((None, rows, rank), seq_map),
            scratch_shapes=[pltpu.VMEM((rows, 1), F32), pltpu.VMEM((rows, 1), F32),
                            pltpu.VMEM((rows, rank), F32)]),
        compiler_params=_params("arbitrary", "arbitrary"),
        name="mla_sample_attention",
    )(page_table.reshape(-1), q_lat, q_pe, c_new, pe_new,
      *([cache_lat] * PAGES_PER_STEP), *([cache_pe_t] * PAGES_PER_STEP))


def _per_head_kernel(x_ref, w_ref, o_ref, *, transpose_w):
    x = x_ref[...].astype(BF16)
    w = w_ref[...].astype(BF16)
    o_ref[...] = (_dot_nt(x, w) if transpose_w else _dot(x, w)).astype(o_ref.dtype)


def per_head_matmul(x, x_row_blk, rows, in_w, w2d, out_w, *, transpose_w, out_dtype):
    if transpose_w:
        w_spec = pl.BlockSpec((out_w, in_w), lambda h: (0, h))
    else:
        w_spec = pl.BlockSpec((in_w, out_w), lambda h: (0, h))
    return pl.pallas_call(
        functools.partial(_per_head_kernel, transpose_w=transpose_w),
        out_shape=jax.ShapeDtypeStruct((rows, MLA_HEADS * out_w), out_dtype),
        grid=(MLA_HEADS,),
        in_specs=[pl.BlockSpec((rows, in_w), lambda h: (x_row_blk, h)), w_spec],
        out_specs=pl.BlockSpec((rows, out_w), lambda h: (0, h)),
        compiler_params=_params("arbitrary"),
        name="per_head_matmul",
    )(x, w2d)


def _cross_kernel(q_ref, k_ref, v_ref, o_ref, *, scale):
    for h in range(MEM_HEADS):
        sl = slice(h * MEM_HEAD_DIM, (h + 1) * MEM_HEAD_DIM)
        k = k_ref[:, h, :] if len(k_ref.shape) == 3 else k_ref[:, sl]
        v = v_ref[:, h, :] if len(v_ref.shape) == 3 else v_ref[:, sl]
        s = _dot_nt(q_ref[:, sl].astype(BF16), k.astype(BF16)) * scale
        p = jnp.exp(s - jnp.max(s, -1, keepdims=True))
        p = p / jnp.sum(p, -1, keepdims=True)
        o_ref[:, sl] = _dot(p.astype(BF16), v.astype(BF16))


def cross_attention(q, q_row_blk_offset, n_seq, seq_len, k, v, kv_map, kv_block, *, tq):
    per = seq_len // tq
    q_spec = pl.BlockSpec((tq, MEM_WIDTH), lambda b, i: (b * per + i + q_row_blk_offset, 0))
    kv_spec = pl.BlockSpec(kv_block, kv_map)
    return pl.pallas_call(
        functools.partial(_cross_kernel, scale=MEM_SCALE),
        out_shape=jax.ShapeDtypeStruct((n_seq * seq_len, MEM_WIDTH), F32),
        grid=(n_seq, per),
        in_specs=[q_spec, kv_spec, kv_spec],
        out_specs=pl.BlockSpec((tq, MEM_WIDTH), lambda b, i: (b * per + i, 0)),
        compiler_params=_params("arbitrary", "arbitrary"),
        name="cross_attention",
    )(q, k, v)


def _window_kernel(q_ref, kp_ref, kc_ref, vp_ref, vc_ref, o_ref, lse_ref, *, scale):
    i = pl.program_id(1)
    t = q_ref.shape[0]
    qpos = i * t + lax.broadcasted_iota(I32, (t, 2 * t), 0)
    kpos = (i - 1) * t + lax.broadcasted_iota(I32, (t, 2 * t), 1)
    valid = (kpos >= 0) & (kpos <= qpos) & (kpos >= qpos - C_SPAN)
    for kvh in range(C_KV_HEADS):
        ksl = slice(kvh * C_HEAD_DIM, (kvh + 1) * C_HEAD_DIM)
        k = jnp.concatenate([kp_ref[:, ksl], kc_ref[:, ksl]], 0).astype(BF16)
        v = jnp.concatenate([vp_ref[:, ksl], vc_ref[:, ksl]], 0).astype(BF16)
        for r in range(C_REP):
            slot = kvh * C_REP + r
            sl = slice(slot * C_HEAD_DIM, (slot + 1) * C_HEAD_DIM)
            s = _dot_nt(q_ref[:, sl].astype(BF16), k) * scale
            s = jnp.where(valid, s, NEG_INF)
            m = jnp.max(s, -1, keepdims=True)
            e = jnp.exp(s - m)
            den = jnp.sum(e, -1, keepdims=True)
            o_ref[:, sl] = _dot(e.astype(BF16), v) / den
            lse_ref[:, sl] = jnp.broadcast_to(m + jnp.log(den), (t, C_HEAD_DIM))


def window_attention(q, k, v):
    n_cls, t_len, _ = q.shape
    t = C_SPAN
    cur = lambda c, i: (c, i, 0)
    prev = lambda c, i: (c, jnp.maximum(i - 1, 0), 0)
    q_spec = pl.BlockSpec((None, t, C_OUT_WIDTH), cur)
    shape = jax.ShapeDtypeStruct((n_cls, t_len, C_OUT_WIDTH), F32)
    return pl.pallas_call(
        functools.partial(_window_kernel, scale=C_SCALE),
        out_shape=(shape, shape),
        grid=(n_cls, t_len // t),
        in_specs=[q_spec, pl.BlockSpec((None, t, C_KV_WIDTH), prev), pl.BlockSpec((None, t, C_KV_WIDTH), cur),
                  pl.BlockSpec((None, t, C_KV_WIDTH), prev), pl.BlockSpec((None, t, C_KV_WIDTH), cur)],
        out_specs=(q_spec, q_spec),
        compiler_params=_params("arbitrary", "arbitrary"),
        name="window_attention",
    )(q, k, k, v, v)


def _merge_kernel(*refs):
    n = C_GROUPS
    o_refs, l_refs, out_ref = refs[:n], refs[n:2 * n], refs[2 * n]
    lses = [r[...] for r in l_refs]
    m = functools.reduce(jnp.maximum, lses)
    ws = [jnp.exp(l - m) for l in lses]
    tot = functools.reduce(lambda a, b: a + b, ws)
    acc = ws[0] * o_refs[0][...]
    for w, o in zip(ws[1:], o_refs[1:]):
        acc += w * o[...]
    out_ref[...] = (acc / tot).astype(out_ref.dtype)


def merge_groups(outs, lses, *, tm=256):
    rows, width = outs[0].shape
    spec = pl.BlockSpec((tm, width), lambda i: (i, 0))
    return pl.pallas_call(
        _merge_kernel,
        out_shape=jax.ShapeDtypeStruct((rows, width), BF16),
        grid=(rows // tm,),
        in_specs=[spec] * (2 * C_GROUPS),
        out_specs=spec,
        compiler_params=_params("arbitrary"),
        name="merge_groups",
    )(*outs, *lses)


def _dilated_sample_kernel(q_ref, kn_ref, vn_ref, bk_ref, bv_ref, o_ref, nk_ref, nv_ref, *, scale):
    t_new = q_ref.shape[0]
    buf = bk_ref.shape[0]
    rows = C_GROUPS * C_REP * t_new
    per_group = C_REP * t_new

    def row_consts(n_keys):
        rid = lax.broadcasted_iota(I32, (rows, n_keys), 0)
        grp = rid // per_group
        dil = jnp.where(grp == 0, C_PATTERNS[0][1], jnp.where(grp == 1, C_PATTERNS[1][1], C_PATTERNS[2][1]))
        win = jnp.where(grp == 0, C_PATTERNS[0][0], jnp.where(grp == 1, C_PATTERNS[1][0], C_PATTERNS[2][0]))
        return rid % t_new, dil, win

    tok_b, dil_b, win_b = row_consts(buf)
    rel_b = buf + tok_b - lax.broadcasted_iota(I32, (rows, buf), 1)
    valid_b = ((rel_b & (dil_b - 1)) == 0) & (rel_b <= win_b)
    tok_n, dil_n, _ = row_consts(t_new)
    rel_n = tok_n - lax.broadcasted_iota(I32, (rows, t_new), 1)
    valid_n = (rel_n >= 0) & ((rel_n & (dil_n - 1)) == 0)

    for kvh in range(C_KV_HEADS):
        ksl = slice(kvh * C_HEAD_DIM, (kvh + 1) * C_HEAD_DIM)
        pieces = []
        for g in range(C_GROUPS):
            for r in range(C_REP):
                col = (g * C_SLOTS + kvh * C_REP + r) * C_HEAD_DIM
                pieces.append(q_ref[:, col:col + C_HEAD_DIM])
        q = jnp.concatenate(pieces, 0).astype(BF16)
        kb, vb = bk_ref[:, kvh, :].astype(BF16), bv_ref[:, kvh, :].astype(BF16)
        kn, vn = kn_ref[:, ksl].astype(BF16), vn_ref[:, ksl].astype(BF16)
        nk_ref[:, kvh, :] = kn_ref[:, ksl]
        nv_ref[:, kvh, :] = vn_ref[:, ksl]
        s_b = jnp.where(valid_b, _dot_nt(q, kb) * scale, NEG_INF)
        s_n = jnp.where(valid_n, _dot_nt(q, kn) * scale, NEG_INF)
        m = jnp.maximum(jnp.max(s_b, -1, keepdims=True), jnp.max(s_n, -1, keepdims=True))
        e_b, e_n = jnp.exp(s_b - m), jnp.exp(s_n - m)
        den = jnp.sum(e_b, -1, keepdims=True) + jnp.sum(e_n, -1, keepdims=True)
        out = (_dot(e_b.astype(BF16), vb) + _dot(e_n.astype(BF16), vn)) / den
        lse = m + jnp.log(den)
        lg = [lse[g * per_group:(g + 1) * per_group] for g in range(C_GROUPS)]
        mm_ = functools.reduce(jnp.maximum, lg)
        wg = [jnp.exp(l - mm_) for l in lg]
        tot = functools.reduce(lambda a, b: a + b, wg)
        merged = sum(w * out[g * per_group:(g + 1) * per_group] for g, w in enumerate(wg)) / tot
        for r in range(C_REP):
            col = (kvh * C_REP + r) * C_HEAD_DIM
            o_ref[:, col:col + C_HEAD_DIM] = merged[r * t_new:(r + 1) * t_new]


def dilated_sample(qk, h, row_blk_offset, n_seq, t_new, cache_k, cache_v, slid_k, slid_v, layer):
    buf = cache_k.shape[2]
    assert t_new == SUBLANES and buf % t_new == 0
    row = lambda b: (b + row_blk_offset, 0)
    cache_spec = pl.BlockSpec((None, None, buf, C_KV_HEADS, C_HEAD_DIM), lambda b: (layer, b, 0, 0, 0))
    new_spec = pl.BlockSpec((None, None, t_new, C_KV_HEADS, C_HEAD_DIM),
                            lambda b: (layer, b, buf // t_new - 1, 0, 0))
    any_spec = pl.BlockSpec(memory_space=pl.ANY)
    slid_shape = jax.ShapeDtypeStruct(slid_k.shape, F32)

    def kern(q_ref, kn_ref, vn_ref, bk_ref, bv_ref, sk_ref, sv_ref, o_ref, nk_ref, nv_ref):
        del sk_ref, sv_ref
        _dilated_sample_kernel(q_ref, kn_ref, vn_ref, bk_ref, bv_ref, o_ref, nk_ref, nv_ref, scale=C_SCALE)

    return pl.pallas_call(
        kern,
        out_shape=(jax.ShapeDtypeStruct((n_seq * t_new, C_OUT_WIDTH), F32), slid_shape, slid_shape),
        grid=(n_seq,),
        in_specs=[pl.BlockSpec((t_new, C_Q_WIDTH), row),
                  pl.BlockSpec((t_new, C_KV_WIDTH), lambda b: (b + row_blk_offset, C_Q_WIDTH // C_KV_WIDTH)),
                  pl.BlockSpec((t_new, C_KV_WIDTH),
                               lambda b: (b + row_blk_offset, (C_Q_WIDTH + C_KV_WIDTH) // C_KV_WIDTH)),
                  cache_spec, cache_spec, any_spec, any_spec],
        out_specs=(pl.BlockSpec((t_new, C_OUT_WIDTH), lambda b: (b, 0)), new_spec, new_spec),
        input_output_aliases={5: 1, 6: 2},
        compiler_params=_params("arbitrary"),
        name="dilated_sample",
    )(qk, qk, h, cache_k, cache_v, slid_k, slid_v)


def _slide_kernel(k_ref, v_ref, ok_ref, ov_ref, *, t_new):
    buf = k_ref.shape[0]
    for src, dst in ((k_ref, ok_ref), (v_ref, ov_ref)):
        dst[0:buf - t_new] = src[t_new:buf]
        dst[buf - t_new:buf] = jnp.zeros((t_new,) + src.shape[1:], F32)


def slide_windows(cache_k, cache_v, t_new):
    n_layers, n_seq, buf = cache_k.shape[:3]
    spec = pl.BlockSpec((None, None, buf, C_KV_HEADS, C_HEAD_DIM), lambda l, b: (l, b, 0, 0, 0))
    shape = jax.ShapeDtypeStruct(cache_k.shape, F32)
    return pl.pallas_call(
        functools.partial(_slide_kernel, t_new=t_new),
        out_shape=(shape, shape),
        grid=(n_layers, n_seq),
        in_specs=[spec, spec],
        out_specs=(spec, spec),
        compiler_params=_params("arbitrary", "arbitrary"),
        name="slide_windows",
    )(cache_k, cache_v)


def _router_kernel(x_ref, w_ref, o_ref):
    logits = jnp.dot(x_ref[...], w_ref[...], precision=lax.Precision.HIGHEST, preferred_element_type=F32)
    lane = lax.broadcasted_iota(I32, logits.shape, 1)
    m1 = jnp.max(logits, -1, keepdims=True)
    i1 = jnp.min(jnp.where(logits == m1, lane, N_EXPERTS), -1, keepdims=True)
    rest = jnp.where(lane == i1, NEG_INF, logits)
    m2 = jnp.max(rest, -1, keepdims=True)
    i2 = jnp.min(jnp.where(rest == m2, lane, N_EXPERTS), -1, keepdims=True)
    e2 = jnp.exp(m2 - m1)
    g1 = 1.0 / (1.0 + e2)
    g2 = e2 / (1.0 + e2)
    o_ref[...] = jnp.where(lane == 0, i1.astype(F32),
                           jnp.where(lane == 1, i2.astype(F32),
                                     jnp.where(lane == 2, g1, jnp.where(lane == 3, g2, 0.0))))


def router_top2(x, w_router, *, tm=512):
    rows, width = x.shape
    return pl.pallas_call(
        _router_kernel,
        out_shape=jax.ShapeDtypeStruct((rows, N_EXPERTS), F32),
        grid=(rows // tm,),
        in_specs=[pl.BlockSpec((tm, width), lambda i: (i, 0)),
                  pl.BlockSpec((width, N_EXPERTS), lambda i: (0, 0))],
        out_specs=pl.BlockSpec((tm, N_EXPERTS), lambda i: (i, 0)),
        compiler_params=_params("arbitrary"),
        name="router_top2",
    )(x, w_router)


def _row_copy(src_ref, src_row, dst_ref, dst_row, sem):
    return pltpu.make_async_copy(src_ref.at[pl.ds(src_row, 1)], dst_ref.at[pl.ds(dst_row, 1)], sem)


def _gather_rows(idx_ref, base, n, src_ref, dst_ref, sem):
    def start(r, c):
        _row_copy(src_ref, idx_ref[base + r], dst_ref, r, sem).start()
        return c

    lax.fori_loop(0, n, start, 0)

    def wait(r, c):
        _row_copy(src_ref, 0, dst_ref, r, sem).wait()
        return c

    lax.fori_loop(0, n, wait, 0)


def _sort_gather_kernel(idx_ref, x_ref, o_ref, buf_ref, sem):
    tm = o_ref.shape[0]
    _gather_rows(idx_ref, pl.program_id(0) * tm, tm, x_ref, buf_ref, sem)
    o_ref[...] = buf_ref[...].astype(o_ref.dtype)


def sort_gather(x, src_rows, *, tm=256):
    n_out = src_rows.shape[0]
    width = x.shape[1]
    return pl.pallas_call(
        _sort_gather_kernel,
        out_shape=jax.ShapeDtypeStruct((n_out, width), BF16),
        grid_spec=pltpu.PrefetchScalarGridSpec(
            num_scalar_prefetch=1, grid=(n_out // tm,),
            in_specs=[pl.BlockSpec(memory_space=pl.ANY)],
            out_specs=pl.BlockSpec((tm, width), lambda i, idx: (i, 0)),
            scratch_shapes=[pltpu.VMEM((tm, width), x.dtype), pltpu.SemaphoreType.DMA]),
        compiler_params=_params("arbitrary"),
        name="sort_gather",
    )(src_rows, x)


def _combine_ln_kernel(idx_ref, y_ref, x_ref, info_ref, g_ref, b_ref, o_ref, ob_ref, y0_ref, y1_ref, sem):
    tm = x_ref.shape[0]
    n_tok = pl.num_programs(0) * tm
    base = pl.program_id(0) * tm
    _gather_rows(idx_ref, base, tm, y_ref, y0_ref, sem)
    _gather_rows(idx_ref, n_tok + base, tm, y_ref, y1_ref, sem)
    info = info_ref[...]
    f = info[:, 2:3] * y0_ref[...] + info[:, 3:4] * y1_ref[...]
    y = _layer_norm_rows(DEEPNORM_ALPHA * x_ref[...] + f, g_ref[...], b_ref[...])
    o_ref[...] = y
    ob_ref[...] = y.astype(BF16)


def combine_ln(y_sorted, dest, x, info, g, b, *, tm=128):
    rows, width = x.shape
    row = pl.BlockSpec((tm, width), lambda i, idx: (i, 0))
    vec = pl.BlockSpec((1, width), lambda i, idx: (0, 0))
    return pl.pallas_call(
        _combine_ln_kernel,
        out_shape=(jax.ShapeDtypeStruct((rows, width), F32), jax.ShapeDtypeStruct((rows, width), BF16)),
        grid_spec=pltpu.PrefetchScalarGridSpec(
            num_scalar_prefetch=1, grid=(rows // tm,),
            in_specs=[pl.BlockSpec(memory_space=pl.ANY), row,
                      pl.BlockSpec((tm, N_EXPERTS), lambda i, idx: (i, 0)), vec, vec],
            out_specs=(row, row),
            scratch_shapes=[pltpu.VMEM((tm, width), F32), pltpu.VMEM((tm, width), F32),
                            pltpu.SemaphoreType.DMA]),
        compiler_params=_params("arbitrary"),
        name="combine_ln",
    )(dest, y_sorted, x, info, g.reshape(1, width), b.reshape(1, width))


MOE_TILE = 512


def moe_routing(info):
    n_tok = info.shape[0]
    n_asg = n_tok * TOP_K
    n_tiles = n_asg // MOE_TILE + N_EXPERTS
    expert = info[:, :TOP_K].astype(I32).T.reshape(-1)
    onehot = (expert[:, None] == jnp.arange(N_EXPERTS, dtype=I32)[None, :]).astype(I32)
    counts = jnp.sum(onehot, 0)
    rank = jnp.sum((jnp.cumsum(onehot, 0) - onehot) * onehot, 1)
    padded = ((counts + MOE_TILE - 1) // MOE_TILE) * MOE_TILE
    ends = jnp.cumsum(padded)
    dest = (ends - padded)[expert] + rank
    token = jnp.arange(n_asg, dtype=I32) % n_tok
    src_rows = jnp.zeros((n_tiles * MOE_TILE,), I32).at[dest].set(token)
    tile_start = jnp.arange(n_tiles, dtype=I32) * MOE_TILE
    tile_expert = jnp.minimum(jnp.sum((tile_start[:, None] >= ends[None, :]).astype(I32), 1), N_EXPERTS - 1)
    return src_rows, dest.astype(I32), tile_expert.astype(I32)


def moe_block(x, w_router, w_gate, w_up, w_down, layer, g, b):
    info = router_top2(x, w_router)
    src_rows, dest, tile_expert = moe_routing(info)
    tile_expert = tile_expert + layer * N_EXPERTS
    xs = sort_gather(x, src_rows)
    hs = gmm(xs, (w_gate, w_up), tile_expert, tm=MOE_TILE, tn=256, out_dtype=BF16, epilogue="swiglu",
             name="moe_up")
    ys = gmm(hs, (w_down,), tile_expert, tm=MOE_TILE, tn=512, out_dtype=F32, name="moe_down")
    return combine_ln(ys, dest, x, info, g, b)


def _rope_tables(pos, half):
    inv = ROPE_THETA ** (-jnp.arange(half, dtype=F32) / half)
    ang = pos.astype(F32)[:, None] * inv[None, :]
    return jnp.cos(ang), jnp.sin(ang)


def _rot_half_cols(w, hd):
    lead = w.shape[:-1]
    wh = w.reshape(lead + (-1, 2, hd // 2))
    return jnp.stack([-wh[..., 1, :], wh[..., 0, :]], -2).reshape(w.shape)


def _dense(x, w, layer=0, *, tn, out_dtype, tm=1024, **kw):
    return gmm(x, (w,), tm=min(tm, x.shape[0]), tn=tn, out_dtype=out_dtype, group=layer, **kw)


def kernel(x_prompt, x_sample, mem_prompt, cache_mla_lat, cache_mla_pe, cache_c_k, cache_c_v, cache_mem_k, cache_mem_v, state_s5_re, state_s5_im, page_table, ln_g, ln_b, w_in_even, s5_lambda_re, s5_lambda_im, s5_log_dt, s5_b_re, s5_b_im, s5_c_re, s5_c_im, s5_d, s5_w_glu, mla_q_norm, mla_kv_norm, mla_w_uq, mla_w_uk, mla_w_uv, w_out_even, w_in_odd, w_out_odd, mem_w_q, mem_w_k, mem_w_v, mem_w_o, ffn_w_gate, ffn_w_up, ffn_w_down, moe_w_router, moe_w_gate, moe_w_up, moe_w_down):
    bp, tp, d = x_prompt.shape
    bs, ts, _ = x_sample.shape
    n_p, n_s = bp * tp, bs * ts
    n_tok = n_p + n_s
    n_pages = page_table.shape[1]
    past_len = n_pages * PAGE_SIZE
    mem_len = mem_prompt.shape[1]
    c_buf = cache_c_k.shape[2]
    ROW = 1024
    assert n_p % ROW == 0 and n_s % ROW == 0 and ts == SUBLANES

    x = jnp.concatenate([x_prompt.reshape(n_p, d), x_sample.reshape(n_s, d)], 0)
    xb = x.astype(BF16)
    mem_b = mem_prompt.reshape(bp * mem_len, d)

    pos = jnp.concatenate([jnp.tile(jnp.arange(tp, dtype=I32), bp),
                           jnp.tile(past_len + jnp.arange(ts, dtype=I32), bs)])
    cos32, sin32 = _rope_tables(pos, MLA_ROPE // 2)
    zeros64 = jnp.zeros((n_tok, LANES - MLA_ROPE), F32)
    cos_mla = jnp.concatenate([cos32, cos32, zeros64], 1)
    sin_mla = jnp.concatenate([sin32, sin32, zeros64], 1)
    cos64, sin64 = _rope_tables(pos, C_HEAD_DIM // 2)
    cos_c = jnp.concatenate([cos64, cos64], 1)
    sin_c = jnp.concatenate([-sin64, sin64], 1)

    cache_pe_t = jnp.swapaxes(cache_mla_pe, 2, 3)
    slid_k, slid_v = slide_windows(cache_c_k, cache_c_v, ts)

    lat_p, pe_p, lat_s, pe_s = [], [], [], []
    s5r_p, s5i_p, s5r_s, s5i_s = [], [], [], []
    ck_p, cv_p, ck_s, cv_s = [], [], [], []
    mk_p, mv_p = [], []

    for l in range(DEPTH):
        if l % 2 == 0:
            e = l // 2
            h = _dense(xb, w_in_even, e, tn=512, out_dtype=F32, name="even_in")

            consts = s5_constants(s5_lambda_re[e], s5_lambda_im[e], s5_log_dt[e], s5_b_re[e], s5_b_im[e],
                                  s5_c_re[e], s5_c_im[e], s5_d[e])
            zero_state = jnp.zeros((bp, S5_FLAT), F32)
            y_p, hr_p, hi_p = s5_core(h, 0, bp, tp, zero_state, zero_state, consts, reset=False, tile_rows=128)
            y_s, hr_s, hi_s = s5_core(h, n_p // 128, bs, ts, state_s5_re[e].reshape(bs, S5_FLAT),
                                      state_s5_im[e].reshape(bs, S5_FLAT), consts, reset=True, tile_rows=128)
            s5r_p.append(hr_p.reshape(bp, S5_GROUPS, S5_STATE))
            s5i_p.append(hi_p.reshape(bp, S5_GROUPS, S5_STATE))
            s5r_s.append(hr_s.reshape(bs, S5_GROUPS, S5_STATE))
            s5i_s.append(hi_s.reshape(bs, S5_GROUPS, S5_STATE))
            y = jnp.concatenate([y_p, y_s], 0)
            s5_out = gmm(y, (s5_w_glu,), tm=ROW, tn=512, out_dtype=BF16, epilogue="glu", extras=(y,),
                         group=e, name="s5_glu")

            cqn, ckv = mla_norms(h, mla_q_norm[e], mla_kv_norm[e])
            kr = jnp.pad(h[:, S5_WIDTH + MLA_Q_RANK + MLA_KV_RANK:], ((0, 0), (0, LANES - MLA_ROPE)))
            kr_rot = jnp.concatenate([-kr[:, MLA_ROPE // 2:MLA_ROPE], kr[:, :MLA_ROPE // 2],
                                      kr[:, MLA_ROPE:]], 1)
            k_pe = rope_pair(kr, 0, kr_rot, 0, LANES, cos_mla, sin_mla, out_dtype=F32)
            lat_p.append(ckv[:n_p].reshape(bp, tp, MLA_KV_RANK))
            lat_s.append(ckv[n_p:].reshape(bs, ts, MLA_KV_RANK))
            pe_p.append(k_pe[:n_p, :MLA_ROPE].reshape(bp, tp, MLA_ROPE))
            pe_s.append(k_pe[n_p:, :MLA_ROPE].reshape(bs, ts, MLA_ROPE))

            w_uq = mla_w_uq[e]
            w_nope = w_uq[:, :, :MLA_NOPE].reshape(MLA_Q_RANK, MLA_HEADS * MLA_NOPE)
            w_pe = w_uq[:, :, MLA_NOPE:]
            pad = ((0, 0), (0, 0), (0, LANES - MLA_ROPE))
            w_pe_pad = jnp.pad(w_pe, pad).reshape(MLA_Q_RANK, MLA_HEADS * LANES)
            w_rot_pad = jnp.pad(_rot_half_cols(w_pe.reshape(MLA_Q_RANK, -1), MLA_ROPE)
                                .reshape(MLA_Q_RANK, MLA_HEADS, MLA_ROPE), pad).reshape(MLA_Q_RANK, -1)
            w_q_ext = jnp.concatenate([w_nope, w_pe_pad, w_rot_pad], 1)
            q_ext = _dense(cqn, w_q_ext, tn=512, out_dtype=F32, name="mla_q")
            hw = MLA_HEADS * LANES
            q_pe = rope_pair(q_ext, 1, q_ext, 2, hw, cos_mla, sin_mla, out_dtype=BF16)

            ckv_b = ckv.astype(BF16)
            w_uk2 = mla_w_uk[e].reshape(MLA_KV_RANK, MLA_HEADS * MLA_NOPE)
            w_uv2 = mla_w_uv[e].reshape(MLA_KV_RANK, MLA_HEADS * MLA_V)
            k_nope = _dense(ckv_b, w_uk2, tn=512, out_dtype=BF16, n_rows=n_p, name="mla_k_up")
            v_full = _dense(ckv_b, w_uv2, tn=512, out_dtype=BF16, n_rows=n_p, name="mla_v_up")
            o_p = mla_prompt_attention(q_ext, q_pe, k_nope, k_pe, v_full, bp, tp, scale=MLA_SCALE)

            q_lat = per_head_matmul(q_ext, n_p // n_s, n_s, MLA_NOPE, w_uk2, MLA_KV_RANK,
                                    transpose_w=True, out_dtype=BF16)
            q_lat = q_lat.reshape(bs, ts * MLA_HEADS, MLA_KV_RANK)
            q_pe_s = q_pe[n_p:].reshape(bs, ts * MLA_HEADS, LANES)[:, :, :MLA_ROPE]
            o_lat = mla_sample_attention(page_table, q_lat, q_pe_s, lat_s[-1], pe_s[-1],
                                         cache_mla_lat, cache_pe_t, e, scale=MLA_SCALE)
            o_lat = o_lat.reshape(n_s, MLA_HEADS * MLA_KV_RANK)
            o_s = per_head_matmul(o_lat, 0, n_s, MLA_KV_RANK, w_uv2, MLA_V, transpose_w=False, out_dtype=BF16)
            mla_out = jnp.concatenate([o_p, o_s], 0)
            mix = _dense(jnp.concatenate([s5_out, mla_out], 1), w_out_even, e, tn=512, out_dtype=F32,
                         name="even_out")
        else:
            o = l // 2
            h = _dense(xb, w_in_odd, o, tn=512, out_dtype=F32, name="odd_in")
            qk = rope_roll(h, C_Q_WIDTH + C_KV_WIDTH, cos_c, sin_c)
            k_p = qk[:n_p, C_Q_WIDTH:]
            v_p = h[:n_p, C_Q_WIDTH + C_KV_WIDTH:]
            keep = min(C_PATTERNS[-1][0], tp)
            ck_p.append(k_p.reshape(bp, tp, C_KV_HEADS, C_HEAD_DIM)[:, tp - keep:])
            cv_p.append(v_p.reshape(bp, tp, C_KV_HEADS, C_HEAD_DIM)[:, tp - keep:])
            outs, lses = [], []
            for g, (_, dil) in enumerate(C_PATTERNS):
                def classes(a, width):
                    a = a.reshape(bp, tp // dil, dil, width)
                    return jnp.transpose(a, (0, 2, 1, 3)).reshape(bp * dil, tp // dil, width)
                qg = classes(qk[:n_p, g * C_OUT_WIDTH:(g + 1) * C_OUT_WIDTH], C_OUT_WIDTH)
                og, lg = window_attention(qg, classes(k_p, C_KV_WIDTH), classes(v_p, C_KV_WIDTH))
                def unclasses(a):
                    a = a.reshape(bp, dil, tp // dil, C_OUT_WIDTH)
                    return jnp.transpose(a, (0, 2, 1, 3)).reshape(n_p, C_OUT_WIDTH)
                outs.append(unclasses(og))
                lses.append(unclasses(lg))
            o_p = merge_groups(outs, lses)
            o_s, slid_k, slid_v = dilated_sample(qk, h, n_p // ts, bs, ts, cache_c_k, cache_c_v,
                                                 slid_k, slid_v, o)
            attn = jnp.concatenate([o_p, o_s.astype(BF16)], 0)
            mix = _dense(attn, w_out_odd, o, tn=512, out_dtype=F32, name="odd_out")

        x, xb = residual_ln(x, mix, ln_g[l, 0], ln_b[l, 0])

        mk = _dense(mem_b, mem_w_k, l, tn=512, out_dtype=F32, name="mem_k")
        mv = _dense(mem_b, mem_w_v, l, tn=512, out_dtype=F32, name="mem_v")
        mk_p.append(mk.reshape(bp, mem_len, MEM_HEADS, MEM_HEAD_DIM))
        mv_p.append(mv.reshape(bp, mem_len, MEM_HEADS, MEM_HEAD_DIM))
        q_mem = _dense(xb, mem_w_q, l, tn=512, out_dtype=F32, name="mem_q")
        ca_p = cross_attention(q_mem, 0, bp, tp, mk.reshape(bp, mem_len, MEM_WIDTH),
                               mv.reshape(bp, mem_len, MEM_WIDTH), lambda b, i: (b, 0, 0),
                               (None, mem_len, MEM_WIDTH), tq=512)
        ca_s = cross_attention(q_mem, n_p // ts, bs, ts, cache_mem_k, cache_mem_v,
                               lambda b, i, l=l: (l, b, 0, 0, 0),
                               (None, None, mem_len, MEM_HEADS, MEM_HEAD_DIM), tq=ts)
        ca = jnp.concatenate([ca_p, ca_s], 0)
        f = _dense(ca, mem_w_o, l, tn=512, out_dtype=F32, name="mem_o")
        x, xb = residual_ln(x, f, ln_g[l, 1], ln_b[l, 1])

        if l % 2 == 0:
            e = l // 2
            hid = gmm(xb, (ffn_w_gate, ffn_w_up), tm=ROW, tn=256, out_dtype=BF16, epilogue="swiglu",
                      group=e, name="ffn_up")
            f = mm(hid, ffn_w_down, e, tm=ROW, tn=1024, tk=1024, out_dtype=F32, name="ffn_down")
            x, xb = residual_ln(x, f, ln_g[l, 2], ln_b[l, 2])
        else:
            o = l // 2
            x, xb = moe_block(x, moe_w_router[o], moe_w_gate, moe_w_up, moe_w_down, o,
                              ln_g[l, 2], ln_b[l, 2])

    return (x[:n_p].reshape(bp, tp, d), x[n_p:].reshape(bs, ts, d),
            jnp.stack(lat_p), jnp.stack(pe_p), jnp.stack(lat_s), jnp.stack(pe_s),
            jnp.stack(s5r_p), jnp.stack(s5i_p), jnp.stack(s5r_s), jnp.stack(s5i_s),
            jnp.stack(ck_p), jnp.stack(cv_p), slid_k, slid_v,
            jnp.stack(mk_p), jnp.stack(mv_p))
```

```python
import functools
import math

import jax
import jax.numpy as jnp
from jax import lax
from jax.experimental import pallas as pl
from jax.experimental.pallas import tpu as pltpu

F32 = jnp.float32
BF16 = jnp.bfloat16
I32 = jnp.int32

VMEM_LIMIT_BYTES = 56 * 1024 * 1024
LANES = 128
SUBLANES = 8

D_MODEL = 4096
DEPTH = 4
PAGE_SIZE = 128
S5_WIDTH = D_MODEL // 2
S5_GROUP = 16
S5_GROUPS = S5_WIDTH // S5_GROUP
S5_STATE = 64
S5_FLAT = S5_GROUPS * S5_STATE
S5_SUPER = 16
S5_NSUPER = S5_GROUPS // S5_SUPER
MLA_HEADS = 16
MLA_NOPE = 128
MLA_ROPE = 64
MLA_V = 128
MLA_KV_RANK = 512
MLA_Q_RANK = D_MODEL // 4
MLA_WIDTH = MLA_HEADS * MLA_V
MLA_SCALE = (MLA_NOPE + MLA_ROPE) ** -0.5
C_HEAD_DIM = 128
C_PATTERNS = ((128, 1), (512, 4), (2048, 16))
C_GROUPS = len(C_PATTERNS)
C_SLOTS = 8
C_KV_HEADS = 4
C_REP = C_SLOTS // C_KV_HEADS
C_SPAN = 128
C_Q_WIDTH = C_GROUPS * C_SLOTS * C_HEAD_DIM
C_KV_WIDTH = C_KV_HEADS * C_HEAD_DIM
C_OUT_WIDTH = C_SLOTS * C_HEAD_DIM
C_SCALE = C_HEAD_DIM ** -0.5
MEM_HEADS = 4
MEM_HEAD_DIM = 128
MEM_WIDTH = MEM_HEADS * MEM_HEAD_DIM
MEM_SCALE = MEM_HEAD_DIM ** -0.5
N_EXPERTS = 8
TOP_K = 2
ROPE_THETA = 10000.0
LN_EPS = 1e-5
RMS_EPS = 1e-6
DEEPNORM_ALPHA = (2.0 * DEPTH) ** 0.25
NEG_INF = float("-inf")

assert all(w // d == C_SPAN for w, d in C_PATTERNS)


def _params(*sem):
    return pltpu.CompilerParams(dimension_semantics=sem, vmem_limit_bytes=VMEM_LIMIT_BYTES)


def _dot(a, b):
    return jnp.dot(a, b, preferred_element_type=F32)


def _dot_nt(a, b):
    return lax.dot_general(a, b, (((1,), (1,)), ((), ())), preferred_element_type=F32)


def _gmm_kernel(tg_ref, x_ref, *refs, n_w, epilogue, n_extra, row_chunk):
    w_refs = refs[:n_w]
    extra_refs = refs[n_w:n_w + n_extra]
    o_ref = refs[n_w + n_extra]
    wb_refs = refs[n_w + n_extra + 1:]
    i = pl.program_id(1)
    prev = tg_ref[jnp.maximum(i - 1, 0)]
    changed = jnp.logical_or(i == 0, tg_ref[i] != prev)

    @pl.when(changed)
    def _():
        depth = w_refs[0].shape[0]
        step = 512 if depth % 512 == 0 else depth
        for w_ref, wb_ref in zip(w_refs, wb_refs):
            for r in range(0, depth, step):
                wb_ref[r:r + step, :] = w_ref[r:r + step, :].astype(BF16)

    tm = x_ref.shape[0]
    for r in range(0, tm, row_chunk):
        x = x_ref[r:r + row_chunk, :].astype(BF16)
        acc = _dot(x, wb_refs[0][...])
        if epilogue == "swiglu":
            up = _dot(x, wb_refs[1][...])
            acc = acc * jax.nn.sigmoid(acc) * up
        elif epilogue == "glu":
            acc = extra_refs[0][r:r + row_chunk, :] * jax.nn.sigmoid(acc)
        o_ref[r:r + row_chunk, :] = acc.astype(o_ref.dtype)


def gmm(x, ws, tile_group=None, *, tm, tn, out_dtype, epilogue="none", extras=(), x_col_block=0,
        row_block_offset=0, n_rows=None, group=0, name="gmm"):
    ws = tuple(w.reshape((-1,) + w.shape[-2:]) for w in ws)
    depth, n_out = ws[0].shape[1], ws[0].shape[2]
    rows = x.shape[0] if n_rows is None else n_rows
    assert rows % tm == 0
    ni, nj = rows // tm, pl.cdiv(n_out, tn)
    if tile_group is None:
        tile_group = jnp.full((ni,), group, I32)
    row_chunk = min(tm, 256)
    kern = functools.partial(_gmm_kernel, n_w=len(ws), epilogue=epilogue, n_extra=len(extras),
                             row_chunk=row_chunk)
    in_specs = [pl.BlockSpec((tm, depth), lambda j, i, tg: (i + row_block_offset, x_col_block))]
    in_specs += [pl.BlockSpec((None, depth, tn), lambda j, i, tg: (tg[i], 0, j)) for _ in ws]
    in_specs += [pl.BlockSpec((tm, tn), lambda j, i, tg: (i + row_block_offset, j)) for _ in extras]
    return pl.pallas_call(
        kern,
        out_shape=jax.ShapeDtypeStruct((rows, n_out), out_dtype),
        grid_spec=pltpu.PrefetchScalarGridSpec(
            num_scalar_prefetch=1, grid=(nj, ni), in_specs=in_specs,
            out_specs=pl.BlockSpec((tm, tn), lambda j, i, tg: (i, j)),
            scratch_shapes=[pltpu.VMEM((depth, tn), BF16) for _ in ws]),
        compiler_params=_params("arbitrary", "arbitrary"),
        name=name,
    )(tile_group, x, *ws, *extras)


def _mm_kernel(x_ref, w_ref, o_ref, acc_ref, *, depth, tk):
    k = pl.program_id(2)
    nk = pl.num_programs(2)

    @pl.when(k == 0)
    def _():
        acc_ref[...] = jnp.zeros_like(acc_ref)

    rem = depth % tk

    def full():
        acc_ref[...] += _dot(x_ref[...].astype(BF16), w_ref[...].astype(BF16))

    if rem == 0:
        full()
    else:
        pl.when(k < nk - 1)(full)

        @pl.when(k == nk - 1)
        def _():
            acc_ref[...] += _dot(x_ref[:, :rem].astype(BF16), w_ref[:rem, :].astype(BF16))

    @pl.when(k == nk - 1)
    def _():
        o_ref[...] = acc_ref[...].astype(o_ref.dtype)


def mm(x, w, layer, *, tm, tn, tk, out_dtype, name="mm"):
    rows, depth = x.shape
    n_out = w.shape[2]
    assert rows % tm == 0 and (depth % tk) % LANES == 0
    grid = (rows // tm, pl.cdiv(n_out, tn), pl.cdiv(depth, tk))
    return pl.pallas_call(
        functools.partial(_mm_kernel, depth=depth, tk=tk),
        out_shape=jax.ShapeDtypeStruct((rows, n_out), out_dtype),
        grid=grid,
        in_specs=[pl.BlockSpec((tm, tk), lambda i, j, k: (i, k)),
                  pl.BlockSpec((None, tk, tn), lambda i, j, k: (layer, k, j))],
        out_specs=pl.BlockSpec((tm, tn), lambda i, j, k: (i, j)),
        scratch_shapes=[pltpu.VMEM((tm, tn), F32)],
        compiler_params=_params("arbitrary", "arbitrary", "arbitrary"),
        name=name,
    )(x, w)


def _layer_norm_rows(v, g, b):
    mu = jnp.mean(v, -1, keepdims=True)
    c = v - mu
    var = jnp.mean(c * c, -1, keepdims=True)
    return c * lax.rsqrt(var + LN_EPS) * g + b


def _ln_kernel(x_ref, f_ref, g_ref, b_ref, o_ref, ob_ref):
    v = DEEPNORM_ALPHA * x_ref[...] + f_ref[...].astype(F32)
    y = _layer_norm_rows(v, g_ref[...], b_ref[...])
    o_ref[...] = y
    ob_ref[...] = y.astype(BF16)


def residual_ln(x, f, g, b, *, tm=256):
    rows, width = x.shape
    row = pl.BlockSpec((tm, width), lambda i: (i, 0))
    vec = pl.BlockSpec((1, width), lambda i: (0, 0))
    return pl.pallas_call(
        _ln_kernel,
        out_shape=(jax.ShapeDtypeStruct((rows, width), F32), jax.ShapeDtypeStruct((rows, width), BF16)),
        grid=(rows // tm,),
        in_specs=[row, row, vec, vec],
        out_specs=(row, row),
        compiler_params=_params("arbitrary"),
        name="residual_ln",
    )(x, f, g.reshape(1, width), b.reshape(1, width))


def _rms(v, g):
    return v * lax.rsqrt(jnp.mean(v * v, -1, keepdims=True) + RMS_EPS) * g


def _mla_norm_kernel(cq_ref, ckv_ref, gq_ref, gkv_ref, oq_ref, okv_ref):
    oq_ref[...] = _rms(cq_ref[...], gq_ref[...]).astype(BF16)
    okv_ref[...] = _rms(ckv_ref[...], gkv_ref[...])


def mla_norms(h, q_norm, kv_norm, *, tm=512):
    rows = h.shape[0]
    q_blk = S5_WIDTH // MLA_Q_RANK
    kv_blk = (S5_WIDTH + MLA_Q_RANK) // MLA_KV_RANK
    return pl.pallas_call(
        _mla_norm_kernel,
        out_shape=(jax.ShapeDtypeStruct((rows, MLA_Q_RANK), BF16),
                   jax.ShapeDtypeStruct((rows, MLA_KV_RANK), F32)),
        grid=(rows // tm,),
        in_specs=[pl.BlockSpec((tm, MLA_Q_RANK), lambda i: (i, q_blk)),
                  pl.BlockSpec((tm, MLA_KV_RANK), lambda i: (i, kv_blk)),
                  pl.BlockSpec((1, MLA_Q_RANK), lambda i: (0, 0)),
                  pl.BlockSpec((1, MLA_KV_RANK), lambda i: (0, 0))],
        out_specs=(pl.BlockSpec((tm, MLA_Q_RANK), lambda i: (i, 0)),
                   pl.BlockSpec((tm, MLA_KV_RANK), lambda i: (i, 0))),
        compiler_params=_params("arbitrary"),
        name="mla_norms",
    )(h, h, q_norm.reshape(1, -1), kv_norm.reshape(1, -1))


def _rope_pair_kernel(x_ref, xr_ref, cos_ref, sin_ref, o_ref, *, hd):
    c, s = cos_ref[...], sin_ref[...]
    for h in range(x_ref.shape[1] // hd):
        sl = slice(h * hd, (h + 1) * hd)
        o_ref[:, sl] = (x_ref[:, sl] * c + xr_ref[:, sl] * s).astype(o_ref.dtype)


def rope_pair(src_x, x_blk, src_r, r_blk, width, cos, sin, *, out_dtype, tm=512):
    rows = src_x.shape[0]
    hd = cos.shape[1]
    return pl.pallas_call(
        functools.partial(_rope_pair_kernel, hd=hd),
        out_shape=jax.ShapeDtypeStruct((rows, width), out_dtype),
        grid=(rows // tm,),
        in_specs=[pl.BlockSpec((tm, width), lambda i: (i, x_blk)),
                  pl.BlockSpec((tm, width), lambda i: (i, r_blk)),
                  pl.BlockSpec((tm, hd), lambda i: (i, 0)),
                  pl.BlockSpec((tm, hd), lambda i: (i, 0))],
        out_specs=pl.BlockSpec((tm, width), lambda i: (i, 0)),
        compiler_params=_params("arbitrary"),
        name="rope_pair",
    )(src_x, src_r, cos, sin)


def _rope_roll_kernel(x_ref, cos_ref, sin_ref, o_ref):
    c, s = cos_ref[...], sin_ref[...]
    for h in range(x_ref.shape[1] // LANES):
        sl = slice(h * LANES, (h + 1) * LANES)
        x = x_ref[:, sl]
        o_ref[:, sl] = x * c + pltpu.roll(x, LANES // 2, 1) * s


def rope_roll(src, width, cos, sin_signed, *, tm=256):
    rows = src.shape[0]
    return pl.pallas_call(
        _rope_roll_kernel,
        out_shape=jax.ShapeDtypeStruct((rows, width), F32),
        grid=(rows // tm,),
        in_specs=[pl.BlockSpec((tm, width), lambda i: (i, 0)),
                  pl.BlockSpec((tm, LANES), lambda i: (i, 0)),
                  pl.BlockSpec((tm, LANES), lambda i: (i, 0))],
        out_specs=pl.BlockSpec((tm, width), lambda i: (i, 0)),
        compiler_params=_params("arbitrary"),
        name="rope_roll",
    )(src, cos, sin_signed)


def _s5_param_kernel(lr_ref, li_ref, ldt_ref, bre_ref, bim_ref, pre_ref, pim_ref, bbre_ref, bbim_ref):
    lr, li = lr_ref[...], li_ref[...]
    dt = jnp.exp(ldt_ref[...])
    for k in range(SUBLANES):
        decay = jnp.exp(lr * dt * (k + 1.0))
        pre_ref[k] = decay * jnp.cos(li * dt * (k + 1.0))
        pim_ref[k] = decay * jnp.sin(li * dt * (k + 1.0))
    a_re, a_im = pre_ref[0], pim_ref[0]
    inv_den = 1.0 / (lr * lr + li * li)
    f_re = ((a_re - 1.0) * lr + a_im * li) * inv_den
    f_im = (a_im * lr - (a_re - 1.0) * li) * inv_den
    for c in range(S5_GROUP):
        br, bi = bre_ref[c], bim_ref[c]
        bbre_ref[c] = f_re * br - f_im * bi
        bbim_ref[c] = f_re * bi + f_im * br


def s5_params(lam_re, lam_im, log_dt, b_re, b_im):
    g, p = lam_re.shape
    b_re_t = jnp.transpose(b_re, (2, 0, 1))
    b_im_t = jnp.transpose(b_im, (2, 0, 1))
    return pl.pallas_call(
        _s5_param_kernel,
        out_shape=(jax.ShapeDtypeStruct((SUBLANES, g, p), F32), jax.ShapeDtypeStruct((SUBLANES, g, p), F32),
                   jax.ShapeDtypeStruct((S5_GROUP, g, p), F32), jax.ShapeDtypeStruct((S5_GROUP, g, p), F32)),
        name="s5_params",
    )(lam_re, lam_im, log_dt.reshape(g, 1), b_re_t, b_im_t)


def _block_diag(blocks):
    s, n, r, c = blocks.shape
    eye = jnp.eye(n, dtype=blocks.dtype)
    return (blocks[:, :, :, None, :] * eye[None, :, None, :, None]).reshape(s, n * r, n * c)


def _gelu_tanh(y):
    return 0.5 * y * (1.0 + jnp.tanh(math.sqrt(2.0 / math.pi) * (y + 0.044715 * (y * y * y))))


S5_LANE_CHUNK = 512


def _s5_kernel(u_ref, h0re_ref, h0im_ref, pre_ref, pim_ref, wb_ref, wcre_ref, wcim_ref, d_ref,
               y_ref, hre_ref, him_ref, sre_ref, sim_ref, cre_ref, cim_ref, *, reset):
    rows = u_ref.shape[0]
    n_groups = rows // SUBLANES
    sup_u = S5_SUPER * S5_GROUP
    sup_h = S5_SUPER * S5_STATE

    for s in range(S5_NSUPER):
        ub = u_ref[:, s * sup_u:(s + 1) * sup_u].astype(BF16)
        bu = _dot(ub, wb_ref[s])
        sre_ref[:, s * sup_h:(s + 1) * sup_h] = bu[:, :sup_h]
        sim_ref[:, s * sup_h:(s + 1) * sup_h] = bu[:, sup_h:]

    if not reset:
        @pl.when(pl.program_id(1) == 0)
        def _():
            cre_ref[...] = h0re_ref[...]
            cim_ref[...] = h0im_ref[...]

    row_id = lax.broadcasted_iota(I32, (SUBLANES, S5_LANE_CHUNK), 0)
    for c0 in range(0, S5_FLAT, S5_LANE_CHUNK):
        lanes = slice(c0, c0 + S5_LANE_CHUNK)
        p_re, p_im = pre_ref[:, lanes], pim_ref[:, lanes]

        def body(gi, carry):
            car_re, car_im = carry
            r0 = pl.multiple_of(gi * SUBLANES, SUBLANES)
            x_re = sre_ref[pl.ds(r0, SUBLANES), lanes]
            x_im = sim_ref[pl.ds(r0, SUBLANES), lanes]
            for lvl, d in enumerate((1, 2, 4)):
                a_re = p_re[d - 1:d, :]
                a_im = p_im[d - 1:d, :]
                keep = row_id >= d
                s_re = jnp.where(keep, pltpu.roll(x_re, d, 0), 0.0)
                s_im = jnp.where(keep, pltpu.roll(x_im, d, 0), 0.0)
                x_re, x_im = (x_re + a_re * s_re - a_im * s_im,
                              x_im + a_re * s_im + a_im * s_re)
            if reset:
                car_re = h0re_ref[pl.ds(gi, 1), lanes]
                car_im = h0im_ref[pl.ds(gi, 1), lanes]
            h_re = x_re + p_re * car_re - p_im * car_im
            h_im = x_im + p_re * car_im + p_im * car_re
            sre_ref[pl.ds(r0, SUBLANES), lanes] = h_re
            sim_ref[pl.ds(r0, SUBLANES), lanes] = h_im
            last_re = h_re[SUBLANES - 1:SUBLANES, :]
            last_im = h_im[SUBLANES - 1:SUBLANES, :]
            if reset:
                hre_ref[pl.ds(gi, 1), lanes] = last_re
                him_ref[pl.ds(gi, 1), lanes] = last_im
            return last_re, last_im

        if reset:
            init = (jnp.zeros((1, S5_LANE_CHUNK), F32), jnp.zeros((1, S5_LANE_CHUNK), F32))
        else:
            init = (cre_ref[:, lanes], cim_ref[:, lanes])
        fin_re, fin_im = lax.fori_loop(0, n_groups, body, init)
        if not reset:
            cre_ref[:, lanes] = fin_re
            cim_ref[:, lanes] = fin_im
            hre_ref[:, lanes] = fin_re
            him_ref[:, lanes] = fin_im

    for s in range(S5_NSUPER):
        h_re = sre_ref[:, s * sup_h:(s + 1) * sup_h].astype(BF16)
        h_im = sim_ref[:, s * sup_h:(s + 1) * sup_h].astype(BF16)
        cols = slice(s * sup_u, (s + 1) * sup_u)
        y = _dot(h_re, wcre_ref[s]) + _dot(h_im, wcim_ref[s]) + d_ref[:, cols] * u_ref[:, cols]
        y_ref[:, cols] = _gelu_tanh(y)


def s5_core(h_in, row_block_offset, n_seq, seq_len, h0_re, h0_im, consts, *, reset, tile_rows):
    pow_re, pow_im, wb, wc_re, wc_im, d_skip = consts
    rows = n_seq * seq_len
    const2 = lambda *_: (0, 0)
    const3 = lambda *_: (0, 0, 0)
    w_specs = [pl.BlockSpec(pow_re.shape, const2), pl.BlockSpec(pow_im.shape, const2),
               pl.BlockSpec(wb.shape, const3), pl.BlockSpec(wc_re.shape, const3),
               pl.BlockSpec(wc_im.shape, const3), pl.BlockSpec(d_skip.shape, const2)]
    scratch = [pltpu.VMEM((tile_rows, S5_FLAT), F32), pltpu.VMEM((tile_rows, S5_FLAT), F32),
               pltpu.VMEM((1, S5_FLAT), F32), pltpu.VMEM((1, S5_FLAT), F32)]
    kern = functools.partial(_s5_kernel, reset=reset)
    if reset:
        assert seq_len == SUBLANES and rows % tile_rows == 0
        seqs = tile_rows // SUBLANES
        grid = (rows // tile_rows,)
        u_spec = pl.BlockSpec((tile_rows, S5_WIDTH), lambda i: (i + row_block_offset, 0))
        h_spec = pl.BlockSpec((seqs, S5_FLAT), lambda i: (i, 0))
        y_spec = pl.BlockSpec((tile_rows, S5_WIDTH), lambda i: (i, 0))
        st_shape = jax.ShapeDtypeStruct((n_seq, S5_FLAT), F32)
        sem = ("arbitrary",)
    else:
        assert seq_len % tile_rows == 0
        per = seq_len // tile_rows
        grid = (n_seq, per)
        u_spec = pl.BlockSpec((tile_rows, S5_WIDTH), lambda b, i: (b * per + i + row_block_offset, 0))
        h_spec = pl.BlockSpec((None, 1, S5_FLAT), lambda b, i: (b, 0, 0))
        y_spec = pl.BlockSpec((tile_rows, S5_WIDTH), lambda b, i: (b * per + i, 0))
        st_shape = jax.ShapeDtypeStruct((n_seq, 1, S5_FLAT), F32)
        h0_re = h0_re.reshape(n_seq, 1, S5_FLAT)
        h0_im = h0_im.reshape(n_seq, 1, S5_FLAT)
        sem = ("arbitrary", "arbitrary")
    y, st_re, st_im = pl.pallas_call(
        kern,
        out_shape=(jax.ShapeDtypeStruct((rows, S5_WIDTH), F32), st_shape, st_shape),
        grid=grid,
        in_specs=[u_spec, h_spec, h_spec] + w_specs,
        out_specs=(y_spec, h_spec, h_spec),
        scratch_shapes=scratch,
        compiler_params=_params(*sem),
        name="s5_reset" if reset else "s5_scan",
    )(h_in, h0_re, h0_im, pow_re, pow_im, wb, wc_re, wc_im, d_skip)
    return y, st_re.reshape(n_seq, S5_FLAT), st_im.reshape(n_seq, S5_FLAT)


def s5_constants(lam_re, lam_im, log_dt, b_re, b_im, c_re, c_im, d_skip):
    pow_re, pow_im, bb_re, bb_im = s5_params(lam_re, lam_im, log_dt, b_re, b_im)
    pow_re = pow_re.reshape(SUBLANES, S5_FLAT)
    pow_im = pow_im.reshape(SUBLANES, S5_FLAT)
    def to_blocks(bb):
        return jnp.transpose(bb, (1, 0, 2)).reshape(S5_NSUPER, S5_SUPER, S5_GROUP, S5_STATE)
    wb = jnp.concatenate([_block_diag(to_blocks(bb_re)), _block_diag(to_blocks(bb_im))], -1).astype(BF16)
    def c_blocks(c):
        return jnp.transpose(c, (0, 2, 1)).reshape(S5_NSUPER, S5_SUPER, S5_STATE, S5_GROUP)
    wc_re = _block_diag(c_blocks(c_re)).astype(BF16)
    wc_im = _block_diag(c_blocks(-c_im)).astype(BF16)
    return pow_re, pow_im, wb, wc_re, wc_im, d_skip.reshape(1, S5_WIDTH)


def _flash_kernel(qn_ref, qp_ref, kn_ref, kp_ref, v_ref, o_ref, m_ref, l_ref, acc_ref, *, scale):
    i, j = pl.program_id(2), pl.program_id(3)
    tq, tk = qn_ref.shape[0], kn_ref.shape[0]

    @pl.when(j == 0)
    def _():
        m_ref[...] = jnp.full_like(m_ref, NEG_INF)
        l_ref[...] = jnp.zeros_like(l_ref)
        acc_ref[...] = jnp.zeros_like(acc_ref)

    def step(diagonal):
        q = jnp.concatenate([qn_ref[...].astype(BF16), qp_ref[...].astype(BF16)], -1)
        k = jnp.concatenate([kn_ref[...].astype(BF16), kp_ref[...].astype(BF16)], -1)
        s = _dot_nt(q, k) * scale
        if diagonal:
            qpos = lax.broadcasted_iota(I32, (tq, tk), 0)
            kpos = lax.broadcasted_iota(I32, (tq, tk), 1)
            s = jnp.where(kpos <= qpos, s, NEG_INF)
        m_new = jnp.maximum(m_ref[...], jnp.max(s, -1, keepdims=True))
        corr = jnp.exp(m_ref[...] - m_new)
        p = jnp.exp(s - m_new)
        l_ref[...] = corr * l_ref[...] + jnp.sum(p, -1, keepdims=True)
        acc_ref[...] = corr * acc_ref[...] + _dot(p.astype(BF16), v_ref[...].astype(BF16))
        m_ref[...] = m_new

    pl.when(j < i)(functools.partial(step, False))
    pl.when(j == i)(functools.partial(step, True))

    @pl.when(j == pl.num_programs(3) - 1)
    def _():
        o_ref[...] = (acc_ref[...] / l_ref[...]).astype(o_ref.dtype)


def mla_prompt_attention(q_ext, q_pe, k_nope, k_pe, v, n_seq, seq_len, *, scale, tq=512, tk=512):
    assert tq == tk
    nq, nk = seq_len // tq, seq_len // tk
    hd = LANES

    def qmap(b, h, i, j):
        return (b * nq + i, h)

    def kmap(b, h, i, j):
        return (b * nk + jnp.minimum(j, i), h)

    def kpmap(b, h, i, j):
        return (b * nk + jnp.minimum(j, i), 0)

    return pl.pallas_call(
        functools.partial(_flash_kernel, scale=scale),
        out_shape=jax.ShapeDtypeStruct((n_seq * seq_len, MLA_HEADS * hd), BF16),
        grid=(n_seq, MLA_HEADS, nq, nk),
        in_specs=[pl.BlockSpec((tq, hd), qmap), pl.BlockSpec((tq, hd), qmap),
                  pl.BlockSpec((tk, hd), kmap), pl.BlockSpec((tk, hd), kpmap),
                  pl.BlockSpec((tk, hd), kmap)],
        out_specs=pl.BlockSpec((tq, hd), qmap),
        scratch_shapes=[pltpu.VMEM((tq, 1), F32), pltpu.VMEM((tq, 1), F32), pltpu.VMEM((tq, hd), F32)],
        compiler_params=_params("arbitrary", "arbitrary", "arbitrary", "arbitrary"),
        name="mla_prompt_attention",
    )(q_ext, q_pe, k_nope, k_pe, v)


PAGES_PER_STEP = 8


def _paged_kernel(pt_ref, ql_ref, qp_ref, cn_ref, pn_ref, *refs, scale, heads):
    lat_refs = refs[:PAGES_PER_STEP]
    pe_refs = refs[PAGES_PER_STEP:2 * PAGES_PER_STEP]
    o_ref, m_ref, l_ref, acc_ref = refs[2 * PAGES_PER_STEP:]
    j = pl.program_id(1)
    rows = ql_ref.shape[0]

    @pl.when(j == 0)
    def _():
        m_ref[...] = jnp.full_like(m_ref, NEG_INF)
        l_ref[...] = jnp.zeros_like(l_ref)
        acc_ref[...] = jnp.zeros_like(acc_ref)

    ql, qp = ql_ref[...], qp_ref[...]
    lats = [r[...].astype(BF16) for r in lat_refs]
    s = jnp.concatenate(
        [_dot_nt(ql, lat) + _dot(qp, pe[...].astype(BF16)) for lat, pe in zip(lats, pe_refs)], -1) * scale
    m_new = jnp.maximum(m_ref[...], jnp.max(s, -1, keepdims=True))
    corr = jnp.exp(m_ref[...] - m_new)
    p = jnp.exp(s - m_new)
    l_ref[...] = corr * l_ref[...] + jnp.sum(p, -1, keepdims=True)
    pv = _dot(p[:, :PAGE_SIZE].astype(BF16), lats[0])
    for n in range(1, PAGES_PER_STEP):
        pv += _dot(p[:, n * PAGE_SIZE:(n + 1) * PAGE_SIZE].astype(BF16), lats[n])
    acc_ref[...] = corr * acc_ref[...] + pv
    m_ref[...] = m_new

    @pl.when(j == pl.num_programs(1) - 1)
    def _():
        cn = cn_ref[...].astype(BF16)
        t_new = cn.shape[0]
        sn = (_dot_nt(ql, cn) + _dot_nt(qp, pn_ref[...].astype(BF16))) * scale
        q_tok = lax.broadcasted_iota(I32, (rows, t_new), 0) // heads
        k_tok = lax.broadcasted_iota(I32, (rows, t_new), 1)
        sn = jnp.where(k_tok <= q_tok, sn, NEG_INF)
        m_fin = jnp.maximum(m_ref[...], jnp.max(sn, -1, keepdims=True))
        corr2 = jnp.exp(m_ref[...] - m_fin)
        pn = jnp.exp(sn - m_fin)
        l_fin = corr2 * l_ref[...] + jnp.sum(pn, -1, keepdims=True)
        acc = corr2 * acc_ref[...] + _dot(pn.astype(BF16), cn)
        o_ref[...] = (acc / l_fin).astype(o_ref.dtype)


def mla_sample_attention(page_table, q_lat, q_pe, c_new, pe_new, cache_lat, cache_pe_t, layer, *, scale):
    n_seq, rows, rank = q_lat.shape
    n_pages = page_table.shape[1]
    assert n_pages % PAGES_PER_STEP == 0
    steps = n_pages // PAGES_PER_STEP
    t_new = c_new.shape[1]
    rope = q_pe.shape[2]

    def seq_map(b, j, pt):
        return (b, 0, 0)

    def page_map(n):
        return lambda b, j, pt: (layer, pt[b * n_pages + j * PAGES_PER_STEP + n], 0, 0)

    in_specs = [pl.BlockSpec((None, rows, rank), seq_map), pl.BlockSpec((None, rows, rope), seq_map),
                pl.BlockSpec((None, t_new, rank), seq_map), pl.BlockSpec((None, t_new, rope), seq_map)]
    in_specs += [pl.BlockSpec((None, None, PAGE_SIZE, rank), page_map(n)) for n in range(PAGES_PER_STEP)]
    in_specs += [pl.BlockSpec((None, None, rope, PAGE_SIZE), page_map(n)) for n in range(PAGES_PER_STEP)]
    return pl.pallas_call(
        functools.partial(_paged_kernel, scale=scale, heads=rows // t_new),
        out_shape=jax.ShapeDtypeStruct((n_seq, rows, rank), BF16),
        grid_spec=pltpu.PrefetchScalarGridSpec(
            num_scalar_prefetch=1, grid=(n_seq, steps), in_specs=in_specs,
            out_specs=pl.BlockSpec((None, rows, rank), seq_map),
            scratch_shapes=[pltpu.VMEM((rows, 1), F32), pltpu.VMEM((rows, 1), F32),
                            pltpu.VMEM((rows, rank), F32)]),
        compiler_params=_params("arbitrary", "arbitrary"),
        name="mla_sample_attention",
    )(page_table.reshape(-1), q_lat, q_pe, c_new, pe_new,
      *([cache_lat] * PAGES_PER_STEP), *([cache_pe_t] * PAGES_PER_STEP))


def _per_head_kernel(x_ref, w_ref, o_ref, *, transpose_w):
    x = x_ref[...].astype(BF16)
    w = w_ref[...].astype(BF16)
    o_ref[...] = (_dot_nt(x, w) if transpose_w else _dot(x, w)).astype(o_ref.dtype)


def per_head_matmul(x, x_row_blk, rows, in_w, w2d, out_w, *, transpose_w, out_dtype):
    if transpose_w:
        w_spec = pl.BlockSpec((out_w, in_w), lambda h: (0, h))
    else:
        w_spec = pl.BlockSpec((in_w, out_w), lambda h: (0, h))
    return pl.pallas_call(
        functools.partial(_per_head_kernel, transpose_w=transpose_w),
        out_shape=jax.ShapeDtypeStruct((rows, MLA_HEADS * out_w), out_dtype),
        grid=(MLA_HEADS,),
        in_specs=[pl.BlockSpec((rows, in_w), lambda h: (x_row_blk, h)), w_spec],
        out_specs=pl.BlockSpec((rows, out_w), lambda h: (0, h)),
        compiler_params=_params("arbitrary"),
        name="per_head_matmul",
    )(x, w2d)


def _cross_kernel(q_ref, k_ref, v_ref, o_ref, *, scale):
    for h in range(MEM_HEADS):
        sl = slice(h * MEM_HEAD_DIM, (h + 1) * MEM_HEAD_DIM)
        k = k_ref[:, h, :] if len(k_ref.shape) == 3 else k_ref[:, sl]
        v = v_ref[:, h, :] if len(v_ref.shape) == 3 else v_ref[:, sl]
        s = _dot_nt(q_ref[:, sl].astype(BF16), k.astype(BF16)) * scale
        p = jnp.exp(s - jnp.max(s, -1, keepdims=True))
        p = p / jnp.sum(p, -1, keepdims=True)
        o_ref[:, sl] = _dot(p.astype(BF16), v.astype(BF16))


def cross_attention(q, q_row_blk_offset, n_seq, seq_len, k, v, kv_map, kv_block, *, tq):
    per = seq_len // tq
    q_spec = pl.BlockSpec((tq, MEM_WIDTH), lambda b, i: (b * per + i + q_row_blk_offset, 0))
    kv_spec = pl.BlockSpec(kv_block, kv_map)
    return pl.pallas_call(
        functools.partial(_cross_kernel, scale=MEM_SCALE),
        out_shape=jax.ShapeDtypeStruct((n_seq * seq_len, MEM_WIDTH), F32),
        grid=(n_seq, per),
        in_specs=[q_spec, kv_spec, kv_spec],
        out_specs=pl.BlockSpec((tq, MEM_WIDTH), lambda b, i: (b * per + i, 0)),
        compiler_params=_params("arbitrary", "arbitrary"),
        name="cross_attention",
    )(q, k, v)


def _window_kernel(q_ref, kp_ref, kc_ref, vp_ref, vc_ref, o_ref, lse_ref, *, scale):
    i = pl.program_id(1)
    t = q_ref.shape[0]
    qpos = i * t + lax.broadcasted_iota(I32, (t, 2 * t), 0)
    kpos = (i - 1) * t + lax.broadcasted_iota(I32, (t, 2 * t), 1)
    valid = (kpos >= 0) & (kpos <= qpos) & (kpos >= qpos - C_SPAN)
    for kvh in range(C_KV_HEADS):
        ksl = slice(kvh * C_HEAD_DIM, (kvh + 1) * C_HEAD_DIM)
        k = jnp.concatenate([kp_ref[:, ksl], kc_ref[:, ksl]], 0).astype(BF16)
        v = jnp.concatenate([vp_ref[:, ksl], vc_ref[:, ksl]], 0).astype(BF16)
        for r in range(C_REP):
            slot = kvh * C_REP + r
            sl = slice(slot * C_HEAD_DIM, (slot + 1) * C_HEAD_DIM)
            s = _dot_nt(q_ref[:, sl].astype(BF16), k) * scale
            s = jnp.where(valid, s, NEG_INF)
            m = jnp.max(s, -1, keepdims=True)
            e = jnp.exp(s - m)
            den = jnp.sum(e, -1, keepdims=True)
            o_ref[:, sl] = _dot(e.astype(BF16), v) / den
            lse_ref[:, sl] = jnp.broadcast_to(m + jnp.log(den), (t, C_HEAD_DIM))


def window_attention(q, k, v):
    n_cls, t_len, _ = q.shape
    t = C_SPAN
    cur = lambda c, i: (c, i, 0)
    prev = lambda c, i: (c, jnp.maximum(i - 1, 0), 0)
    q_spec = pl.BlockSpec((None, t, C_OUT_WIDTH), cur)
    shape = jax.ShapeDtypeStruct((n_cls, t_len, C_OUT_WIDTH), F32)
    return pl.pallas_call(
        functools.partial(_window_kernel, scale=C_SCALE),
        out_shape=(shape, shape),
        grid=(n_cls, t_len // t),
        in_specs=[q_spec, pl.BlockSpec((None, t, C_KV_WIDTH), prev), pl.BlockSpec((None, t, C_KV_WIDTH), cur),
                  pl.BlockSpec((None, t, C_KV_WIDTH), prev), pl.BlockSpec((None, t, C_KV_WIDTH), cur)],
        out_specs=(q_spec, q_spec),
        compiler_params=_params("arbitrary", "arbitrary"),
        name="window_attention",
    )(q, k, k, v, v)


def _merge_kernel(*refs):
    n = C_GROUPS
    o_refs, l_refs, out_ref = refs[:n], refs[n:2 * n], refs[2 * n]
    lses = [r[...] for r in l_refs]
    m = functools.reduce(jnp.maximum, lses)
    ws = [jnp.exp(l - m) for l in lses]
    tot = functools.reduce(lambda a, b: a + b, ws)
    acc = ws[0] * o_refs[0][...]
    for w, o in zip(ws[1:], o_refs[1:]):
        acc += w * o[...]
    out_ref[...] = (acc / tot).astype(out_ref.dtype)


def merge_groups(outs, lses, *, tm=256):
    rows, width = outs[0].shape
    spec = pl.BlockSpec((tm, width), lambda i: (i, 0))
    return pl.pallas_call(
        _merge_kernel,
        out_shape=jax.ShapeDtypeStruct((rows, width), BF16),
        grid=(rows // tm,),
        in_specs=[spec] * (2 * C_GROUPS),
        out_specs=spec,
        compiler_params=_params("arbitrary"),
        name="merge_groups",
    )(*outs, *lses)


def _group_span(g, buf):
    window = min(C_PATTERNS[g][0], buf) * C_KV_HEADS
    return ((window + LANES - 1) // LANES) * LANES


def _dilated_sample_kernel(q_ref, kn_ref, vn_ref, bk_ref, bv_ref, o_ref, nk_ref, nv_ref, bias_ref, *, scale):
    t_new = q_ref.shape[0]
    flat = bk_ref.shape[0]
    buf = flat // C_KV_HEADS
    per_head = C_REP * t_new
    rows = C_SLOTS * t_new
    spans = [_group_span(g, buf) for g in range(C_GROUPS)]
    offs = [sum(spans[:g]) for g in range(C_GROUPS)]

    @pl.when(pl.program_id(0) == 0)
    def _():
        for g, (win, dil) in enumerate(C_PATTERNS):
            n = spans[g]
            rid = lax.broadcasted_iota(I32, (rows, n), 0)
            col = lax.broadcasted_iota(I32, (rows, n), 1) + (flat - n)
            rel = buf + rid % t_new - col // C_KV_HEADS
            ok = (col % C_KV_HEADS == rid // per_head) & (rel % dil == 0) & (rel <= win)
            bias_ref[:, offs[g]:offs[g] + n] = jnp.where(ok, 0.0, NEG_INF)

    rid_n = lax.broadcasted_iota(I32, (rows, t_new), 0)
    rel_n = rid_n % t_new - lax.broadcasted_iota(I32, (rows, t_new), 1)

    outs, lses = [], []
    for g, (win, dil) in enumerate(C_PATTERNS):
        n = spans[g]
        pieces = [q_ref[:, (g * C_SLOTS + s) * C_HEAD_DIM:(g * C_SLOTS + s + 1) * C_HEAD_DIM]
                  for s in range(C_SLOTS)]
        q = jnp.concatenate(pieces, 0).astype(BF16)
        kb = bk_ref[flat - n:flat, :].astype(BF16)
        vb = bv_ref[flat - n:flat, :].astype(BF16)
        s_b = _dot_nt(q, kb) * scale + bias_ref[:, offs[g]:offs[g] + n]
        s_new = []
        for kvh in range(C_KV_HEADS):
            hs = slice(kvh * per_head, (kvh + 1) * per_head)
            sn = _dot_nt(q[hs], kn_ref[:, kvh * C_HEAD_DIM:(kvh + 1) * C_HEAD_DIM].astype(BF16)) * scale
            ok = (rel_n[hs] >= 0) & (rel_n[hs] % dil == 0)
            s_new.append(jnp.where(ok, sn, NEG_INF))
        s_n = jnp.concatenate(s_new, 0)
        m = jnp.maximum(jnp.max(s_b, -1, keepdims=True), jnp.max(s_n, -1, keepdims=True))
        e_b, e_n = jnp.exp(s_b - m), jnp.exp(s_n - m)
        den = jnp.sum(e_b, -1, keepdims=True) + jnp.sum(e_n, -1, keepdims=True)
        o_new = jnp.concatenate(
            [_dot(e_n[kvh * per_head:(kvh + 1) * per_head].astype(BF16),
                  vn_ref[:, kvh * C_HEAD_DIM:(kvh + 1) * C_HEAD_DIM].astype(BF16))
             for kvh in range(C_KV_HEADS)], 0)
        outs.append((_dot(e_b.astype(BF16), vb) + o_new) / den)
        lses.append(m + jnp.log(den))

    top = functools.reduce(jnp.maximum, lses)
    ws = [jnp.exp(l - top) for l in lses]
    total = functools.reduce(lambda a, b: a + b, ws)
    merged = sum(w * o for w, o in zip(ws, outs)) / total
    for s in range(C_SLOTS):
        o_ref[:, s * C_HEAD_DIM:(s + 1) * C_HEAD_DIM] = merged[s * t_new:(s + 1) * t_new]
    for kvh in range(C_KV_HEADS):
        ksl = slice(kvh * C_HEAD_DIM, (kvh + 1) * C_HEAD_DIM)
        nk_ref[:, kvh, :] = kn_ref[:, ksl]
        nv_ref[:, kvh, :] = vn_ref[:, ksl]


def dilated_sample(qk, h, row_blk_offset, n_seq, t_new, cache_k, cache_v, slid_k, slid_v, layer):
    n_layers, _, buf = cache_k.shape[:3]
    flat = buf * C_KV_HEADS
    assert t_new == SUBLANES and buf % t_new == 0
    flat_k = cache_k.reshape(n_layers, n_seq, flat, C_HEAD_DIM)
    flat_v = cache_v.reshape(n_layers, n_seq, flat, C_HEAD_DIM)
    row = lambda b: (b + row_blk_offset, 0)
    cache_spec = pl.BlockSpec((None, None, flat, C_HEAD_DIM), lambda b: (layer, b, 0, 0))
    new_spec = pl.BlockSpec((None, None, t_new, C_KV_HEADS, C_HEAD_DIM),
                            lambda b: (layer, b, buf // t_new - 1, 0, 0))
    any_spec = pl.BlockSpec(memory_space=pl.ANY)
    slid_shape = jax.ShapeDtypeStruct(slid_k.shape, F32)
    bias_cols = sum(_group_span(g, buf) for g in range(C_GROUPS))

    def kern(q_ref, kn_ref, vn_ref, bk_ref, bv_ref, sk_ref, sv_ref, o_ref, nk_ref, nv_ref, bias_ref):
        del sk_ref, sv_ref
        _dilated_sample_kernel(q_ref, kn_ref, vn_ref, bk_ref, bv_ref, o_ref, nk_ref, nv_ref, bias_ref,
                               scale=C_SCALE)

    return pl.pallas_call(
        kern,
        out_shape=(jax.ShapeDtypeStruct((n_seq * t_new, C_OUT_WIDTH), F32), slid_shape, slid_shape),
        grid=(n_seq,),
        in_specs=[pl.BlockSpec((t_new, C_Q_WIDTH), row),
                  pl.BlockSpec((t_new, C_KV_WIDTH), lambda b: (b + row_blk_offset, C_Q_WIDTH // C_KV_WIDTH)),
                  pl.BlockSpec((t_new, C_KV_WIDTH),
                               lambda b: (b + row_blk_offset, (C_Q_WIDTH + C_KV_WIDTH) // C_KV_WIDTH)),
                  cache_spec, cache_spec, any_spec, any_spec],
        out_specs=(pl.BlockSpec((t_new, C_OUT_WIDTH), lambda b: (b, 0)), new_spec, new_spec),
        scratch_shapes=[pltpu.VMEM((C_SLOTS * t_new, bias_cols), F32)],
        input_output_aliases={5: 1, 6: 2},
        compiler_params=_params("arbitrary"),
        name="dilated_sample",
    )(qk, qk, h, flat_k, flat_v, slid_k, slid_v)


def _slide_kernel(k_ref, v_ref, ok_ref, ov_ref, *, t_new):
    buf = k_ref.shape[0]
    for src, dst in ((k_ref, ok_ref), (v_ref, ov_ref)):
        dst[0:buf - t_new] = src[t_new:buf]
        dst[buf - t_new:buf] = jnp.zeros((t_new,) + src.shape[1:], F32)


def slide_windows(cache_k, cache_v, t_new):
    n_layers, n_seq, buf = cache_k.shape[:3]
    spec = pl.BlockSpec((None, None, buf, C_KV_HEADS, C_HEAD_DIM), lambda l, b: (l, b, 0, 0, 0))
    shape = jax.ShapeDtypeStruct(cache_k.shape, F32)
    return pl.pallas_call(
        functools.partial(_slide_kernel, t_new=t_new),
        out_shape=(shape, shape),
        grid=(n_layers, n_seq),
        in_specs=[spec, spec],
        out_specs=(spec, spec),
        compiler_params=_params("arbitrary", "arbitrary"),
        name="slide_windows",
    )(cache_k, cache_v)


def _router_kernel(x_ref, w_ref, o_ref):
    logits = jnp.dot(x_ref[...], w_ref[...], precision=lax.Precision.HIGHEST, preferred_element_type=F32)
    lane = lax.broadcasted_iota(I32, logits.shape, 1)
    m1 = jnp.max(logits, -1, keepdims=True)
    i1 = jnp.min(jnp.where(logits == m1, lane, N_EXPERTS), -1, keepdims=True)
    rest = jnp.where(lane == i1, NEG_INF, logits)
    m2 = jnp.max(rest, -1, keepdims=True)
    i2 = jnp.min(jnp.where(rest == m2, lane, N_EXPERTS), -1, keepdims=True)
    e2 = jnp.exp(m2 - m1)
    g1 = 1.0 / (1.0 + e2)
    g2 = e2 / (1.0 + e2)
    o_ref[...] = jnp.where(lane == 0, i1.astype(F32),
                           jnp.where(lane == 1, i2.astype(F32),
                                     jnp.where(lane == 2, g1, jnp.where(lane == 3, g2, 0.0))))


def router_top2(x, w_router, *, tm=512):
    rows, width = x.shape
    return pl.pallas_call(
        _router_kernel,
        out_shape=jax.ShapeDtypeStruct((rows, N_EXPERTS), F32),
        grid=(rows // tm,),
        in_specs=[pl.BlockSpec((tm, width), lambda i: (i, 0)),
                  pl.BlockSpec((width, N_EXPERTS), lambda i: (0, 0))],
        out_specs=pl.BlockSpec((tm, N_EXPERTS), lambda i: (i, 0)),
        compiler_params=_params("arbitrary"),
        name="router_top2",
    )(x, w_router)


def _row_copy(src_ref, src_row, dst_ref, dst_row, sem):
    return pltpu.make_async_copy(src_ref.at[pl.ds(src_row, 1)], dst_ref.at[pl.ds(dst_row, 1)], sem)


def _gather_rows(idx_ref, base, n, src_ref, dst_ref, sem):
    def start(r, c):
        _row_copy(src_ref, idx_ref[base + r], dst_ref, r, sem).start()
        return c

    lax.fori_loop(0, n, start, 0)

    def wait(r, c):
        _row_copy(src_ref, 0, dst_ref, r, sem).wait()
        return c

    lax.fori_loop(0, n, wait, 0)


def _sort_gather_kernel(idx_ref, x_ref, o_ref, buf_ref, sem):
    tm = o_ref.shape[0]
    _gather_rows(idx_ref, pl.program_id(0) * tm, tm, x_ref, buf_ref, sem)
    o_ref[...] = buf_ref[...].astype(o_ref.dtype)


def sort_gather(x, src_rows, *, tm=256):
    n_out = src_rows.shape[0]
    width = x.shape[1]
    return pl.pallas_call(
        _sort_gather_kernel,
        out_shape=jax.ShapeDtypeStruct((n_out, width), BF16),
        grid_spec=pltpu.PrefetchScalarGridSpec(
            num_scalar_prefetch=1, grid=(n_out // tm,),
            in_specs=[pl.BlockSpec(memory_space=pl.ANY)],
            out_specs=pl.BlockSpec((tm, width), lambda i, idx: (i, 0)),
            scratch_shapes=[pltpu.VMEM((tm, width), x.dtype), pltpu.SemaphoreType.DMA]),
        compiler_params=_params("arbitrary"),
        name="sort_gather",
    )(src_rows, x)


def _combine_ln_kernel(idx_ref, y_ref, x_ref, info_ref, g_ref, b_ref, o_ref, ob_ref, y0_ref, y1_ref, sem):
    tm = x_ref.shape[0]
    n_tok = pl.num_programs(0) * tm
    base = pl.program_id(0) * tm
    _gather_rows(idx_ref, base, tm, y_ref, y0_ref, sem)
    _gather_rows(idx_ref, n_tok + base, tm, y_ref, y1_ref, sem)
    info = info_ref[...]
    f = info[:, 2:3] * y0_ref[...] + info[:, 3:4] * y1_ref[...]
    y = _layer_norm_rows(DEEPNORM_ALPHA * x_ref[...] + f, g_ref[...], b_ref[...])
    o_ref[...] = y
    ob_ref[...] = y.astype(BF16)


def combine_ln(y_sorted, dest, x, info, g, b, *, tm=128):
    rows, width = x.shape
    row = pl.BlockSpec((tm, width), lambda i, idx: (i, 0))
    vec = pl.BlockSpec((1, width), lambda i, idx: (0, 0))
    return pl.pallas_call(
        _combine_ln_kernel,
        out_shape=(jax.ShapeDtypeStruct((rows, width), F32), jax.ShapeDtypeStruct((rows, width), BF16)),
        grid_spec=pltpu.PrefetchScalarGridSpec(
            num_scalar_prefetch=1, grid=(rows // tm,),
            in_specs=[pl.BlockSpec(memory_space=pl.ANY), row,
                      pl.BlockSpec((tm, N_EXPERTS), lambda i, idx: (i, 0)), vec, vec],
            out_specs=(row, row),
            scratch_shapes=[pltpu.VMEM((tm, width), F32), pltpu.VMEM((tm, width), F32),
                            pltpu.SemaphoreType.DMA]),
        compiler_params=_params("arbitrary"),
        name="combine_ln",
    )(dest, y_sorted, x, info, g.reshape(1, width), b.reshape(1, width))


MOE_TILE = 512


def moe_routing(info):
    n_tok = info.shape[0]
    n_asg = n_tok * TOP_K
    n_tiles = n_asg // MOE_TILE + N_EXPERTS
    expert = info[:, :TOP_K].astype(I32).T.reshape(-1)
    onehot = (expert[:, None] == jnp.arange(N_EXPERTS, dtype=I32)[None, :]).astype(I32)
    counts = jnp.sum(onehot, 0)
    rank = jnp.sum((jnp.cumsum(onehot, 0) - onehot) * onehot, 1)
    padded = ((counts + MOE_TILE - 1) // MOE_TILE) * MOE_TILE
    ends = jnp.cumsum(padded)
    dest = (ends - padded)[expert] + rank
    token = jnp.arange(n_asg, dtype=I32) % n_tok
    src_rows = jnp.zeros((n_tiles * MOE_TILE,), I32).at[dest].set(token)
    tile_start = jnp.arange(n_tiles, dtype=I32) * MOE_TILE
    tile_expert = jnp.minimum(jnp.sum((tile_start[:, None] >= ends[None, :]).astype(I32), 1), N_EXPERTS - 1)
    return src_rows, dest.astype(I32), tile_expert.astype(I32)


def moe_block(x, w_router, w_gate, w_up, w_down, layer, g, b):
    info = router_top2(x, w_router)
    src_rows, dest, tile_expert = moe_routing(info)
    tile_expert = tile_expert + layer * N_EXPERTS
    xs = sort_gather(x, src_rows)
    hs = gmm(xs, (w_gate, w_up), tile_expert, tm=MOE_TILE, tn=256, out_dtype=BF16, epilogue="swiglu",
             name="moe_up")
    ys = gmm(hs, (w_down,), tile_expert, tm=MOE_TILE, tn=512, out_dtype=F32, name="moe_down")
    return combine_ln(ys, dest, x, info, g, b)


def _rope_tables(pos, half):
    inv = ROPE_THETA ** (-jnp.arange(half, dtype=F32) / half)
    ang = pos.astype(F32)[:, None] * inv[None, :]
    return jnp.cos(ang), jnp.sin(ang)


def _rot_half_cols(w, hd):
    lead = w.shape[:-1]
    wh = w.reshape(lead + (-1, 2, hd // 2))
    return jnp.stack([-wh[..., 1, :], wh[..., 0, :]], -2).reshape(w.shape)


def _dense(x, w, layer=0, *, tn, out_dtype, tm=1024, **kw):
    return gmm(x, (w,), tm=min(tm, x.shape[0]), tn=tn, out_dtype=out_dtype, group=layer, **kw)


def kernel(x_prompt, x_sample, mem_prompt, cache_mla_lat, cache_mla_pe, cache_c_k, cache_c_v, cache_mem_k, cache_mem_v, state_s5_re, state_s5_im, page_table, ln_g, ln_b, w_in_even, s5_lambda_re, s5_lambda_im, s5_log_dt, s5_b_re, s5_b_im, s5_c_re, s5_c_im, s5_d, s5_w_glu, mla_q_norm, mla_kv_norm, mla_w_uq, mla_w_uk, mla_w_uv, w_out_even, w_in_odd, w_out_odd, mem_w_q, mem_w_k, mem_w_v, mem_w_o, ffn_w_gate, ffn_w_up, ffn_w_down, moe_w_router, moe_w_gate, moe_w_up, moe_w_down):
    bp, tp, d = x_prompt.shape
    bs, ts, _ = x_sample.shape
    n_p, n_s = bp * tp, bs * ts
    n_tok = n_p + n_s
    n_pages = page_table.shape[1]
    past_len = n_pages * PAGE_SIZE
    mem_len = mem_prompt.shape[1]
    c_buf = cache_c_k.shape[2]
    ROW = 1024
    assert n_p % ROW == 0 and n_s % ROW == 0 and ts == SUBLANES

    x = jnp.concatenate([x_prompt.reshape(n_p, d), x_sample.reshape(n_s, d)], 0)
    xb = x.astype(BF16)
    mem_b = mem_prompt.reshape(bp * mem_len, d)

    pos = jnp.concatenate([jnp.tile(jnp.arange(tp, dtype=I32), bp),
                           jnp.tile(past_len + jnp.arange(ts, dtype=I32), bs)])
    cos32, sin32 = _rope_tables(pos, MLA_ROPE // 2)
    zeros64 = jnp.zeros((n_tok, LANES - MLA_ROPE), F32)
    cos_mla = jnp.concatenate([cos32, cos32, zeros64], 1)
    sin_mla = jnp.concatenate([sin32, sin32, zeros64], 1)
    cos64, sin64 = _rope_tables(pos, C_HEAD_DIM // 2)
    cos_c = jnp.concatenate([cos64, cos64], 1)
    sin_c = jnp.concatenate([-sin64, sin64], 1)

    cache_pe_t = jnp.swapaxes(cache_mla_pe, 2, 3)
    slid_k, slid_v = slide_windows(cache_c_k, cache_c_v, ts)

    lat_p, pe_p, lat_s, pe_s = [], [], [], []
    s5r_p, s5i_p, s5r_s, s5i_s = [], [], [], []
    ck_p, cv_p, ck_s, cv_s = [], [], [], []
    mk_p, mv_p = [], []

    for l in range(DEPTH):
        if l % 2 == 0:
            e = l // 2
            h = mm(xb, w_in_even, e, tm=ROW, tn=1024, tk=1024, out_dtype=F32, name="even_in")

            consts = s5_constants(s5_lambda_re[e], s5_lambda_im[e], s5_log_dt[e], s5_b_re[e], s5_b_im[e],
                                  s5_c_re[e], s5_c_im[e], s5_d[e])
            zero_state = jnp.zeros((bp, S5_FLAT), F32)
            y_p, hr_p, hi_p = s5_core(h, 0, bp, tp, zero_state, zero_state, consts, reset=False, tile_rows=128)
            y_s, hr_s, hi_s = s5_core(h, n_p // 128, bs, ts, state_s5_re[e].reshape(bs, S5_FLAT),
                                      state_s5_im[e].reshape(bs, S5_FLAT), consts, reset=True, tile_rows=128)
            s5r_p.append(hr_p.reshape(bp, S5_GROUPS, S5_STATE))
            s5i_p.append(hi_p.reshape(bp, S5_GROUPS, S5_STATE))
            s5r_s.append(hr_s.reshape(bs, S5_GROUPS, S5_STATE))
            s5i_s.append(hi_s.reshape(bs, S5_GROUPS, S5_STATE))
            y = jnp.concatenate([y_p, y_s], 0)
            s5_out = gmm(y, (s5_w_glu,), tm=ROW, tn=512, out_dtype=BF16, epilogue="glu", extras=(y,),
                         group=e, name="s5_glu")

            cqn, ckv = mla_norms(h, mla_q_norm[e], mla_kv_norm[e])
            kr = jnp.pad(h[:, S5_WIDTH + MLA_Q_RANK + MLA_KV_RANK:], ((0, 0), (0, LANES - MLA_ROPE)))
            kr_rot = jnp.concatenate([-kr[:, MLA_ROPE // 2:MLA_ROPE], kr[:, :MLA_ROPE // 2],
                                      kr[:, MLA_ROPE:]], 1)
            k_pe = rope_pair(kr, 0, kr_rot, 0, LANES, cos_mla, sin_mla, out_dtype=F32)
            lat_p.append(ckv[:n_p].reshape(bp, tp, MLA_KV_RANK))
            lat_s.append(ckv[n_p:].reshape(bs, ts, MLA_KV_RANK))
            pe_p.append(k_pe[:n_p, :MLA_ROPE].reshape(bp, tp, MLA_ROPE))
            pe_s.append(k_pe[n_p:, :MLA_ROPE].reshape(bs, ts, MLA_ROPE))

            w_uq = mla_w_uq[e]
            w_nope = w_uq[:, :, :MLA_NOPE].reshape(MLA_Q_RANK, MLA_HEADS * MLA_NOPE)
            w_pe = w_uq[:, :, MLA_NOPE:]
            pad = ((0, 0), (0, 0), (0, LANES - MLA_ROPE))
            w_pe_pad = jnp.pad(w_pe, pad).reshape(MLA_Q_RANK, MLA_HEADS * LANES)
            w_rot_pad = jnp.pad(_rot_half_cols(w_pe.reshape(MLA_Q_RANK, -1), MLA_ROPE)
                                .reshape(MLA_Q_RANK, MLA_HEADS, MLA_ROPE), pad).reshape(MLA_Q_RANK, -1)
            w_q_ext = jnp.concatenate([w_nope, w_pe_pad, w_rot_pad], 1)
            q_ext = _dense(cqn, w_q_ext, tn=512, out_dtype=F32, name="mla_q")
            hw = MLA_HEADS * LANES
            q_pe = rope_pair(q_ext, 1, q_ext, 2, hw, cos_mla, sin_mla, out_dtype=BF16)

            ckv_b = ckv.astype(BF16)
            w_uk2 = mla_w_uk[e].reshape(MLA_KV_RANK, MLA_HEADS * MLA_NOPE)
            w_uv2 = mla_w_uv[e].reshape(MLA_KV_RANK, MLA_HEADS * MLA_V)
            k_nope = _dense(ckv_b, w_uk2, tn=512, out_dtype=BF16, n_rows=n_p, name="mla_k_up")
            v_full = _dense(ckv_b, w_uv2, tn=512, out_dtype=BF16, n_rows=n_p, name="mla_v_up")
            o_p = mla_prompt_attention(q_ext, q_pe, k_nope, k_pe, v_full, bp, tp, scale=MLA_SCALE)

            q_lat = per_head_matmul(q_ext, n_p // n_s, n_s, MLA_NOPE, w_uk2, MLA_KV_RANK,
                                    transpose_w=True, out_dtype=BF16)
            q_lat = q_lat.reshape(bs, ts * MLA_HEADS, MLA_KV_RANK)
            q_pe_s = q_pe[n_p:].reshape(bs, ts * MLA_HEADS, LANES)[:, :, :MLA_ROPE]
            o_lat = mla_sample_attention(page_table, q_lat, q_pe_s, lat_s[-1], pe_s[-1],
                                         cache_mla_lat, cache_pe_t, e, scale=MLA_SCALE)
            o_lat = o_lat.reshape(n_s, MLA_HEADS * MLA_KV_RANK)
            o_s = per_head_matmul(o_lat, 0, n_s, MLA_KV_RANK, w_uv2, MLA_V, transpose_w=False, out_dtype=BF16)
            mla_out = jnp.concatenate([o_p, o_s], 0)
            mix = mm(jnp.concatenate([s5_out, mla_out], 1), w_out_even, e, tm=ROW, tn=1024, tk=1024,
                     out_dtype=F32, name="even_out")
        else:
            o = l // 2
            h = mm(xb, w_in_odd, o, tm=ROW, tn=1024, tk=1024, out_dtype=F32, name="odd_in")
            qk = rope_roll(h, C_Q_WIDTH + C_KV_WIDTH, cos_c, sin_c)
            k_p = qk[:n_p, C_Q_WIDTH:]
            v_p = h[:n_p, C_Q_WIDTH + C_KV_WIDTH:]
            keep = min(C_PATTERNS[-1][0], tp)
            ck_p.append(k_p.reshape(bp, tp, C_KV_HEADS, C_HEAD_DIM)[:, tp - keep:])
            cv_p.append(v_p.reshape(bp, tp, C_KV_HEADS, C_HEAD_DIM)[:, tp - keep:])
            outs, lses = [], []
            for g, (_, dil) in enumerate(C_PATTERNS):
                def classes(a, width):
                    a = a.reshape(bp, tp // dil, dil, width)
                    return jnp.transpose(a, (0, 2, 1, 3)).reshape(bp * dil, tp // dil, width)
                qg = classes(qk[:n_p, g * C_OUT_WIDTH:(g + 1) * C_OUT_WIDTH], C_OUT_WIDTH)
                og, lg = window_attention(qg, classes(k_p, C_KV_WIDTH), classes(v_p, C_KV_WIDTH))
                def unclasses(a):
                    a = a.reshape(bp, dil, tp // dil, C_OUT_WIDTH)
                    return jnp.transpose(a, (0, 2, 1, 3)).reshape(n_p, C_OUT_WIDTH)
                outs.append(unclasses(og))
                lses.append(unclasses(lg))
            o_p = merge_groups(outs, lses)
            o_s, slid_k, slid_v = dilated_sample(qk, h, n_p // ts, bs, ts, cache_c_k, cache_c_v,
                                                 slid_k, slid_v, o)
            attn = jnp.concatenate([o_p, o_s.astype(BF16)], 0)
            mix = _dense(attn, w_out_odd, o, tn=512, out_dtype=F32, name="odd_out")

        x, xb = residual_ln(x, mix, ln_g[l, 0], ln_b[l, 0])

        mk = _dense(mem_b, mem_w_k, l, tn=512, out_dtype=F32, name="mem_k")
        mv = _dense(mem_b, mem_w_v, l, tn=512, out_dtype=F32, name="mem_v")
        mk_p.append(mk.reshape(bp, mem_len, MEM_HEADS, MEM_HEAD_DIM))
        mv_p.append(mv.reshape(bp, mem_len, MEM_HEADS, MEM_HEAD_DIM))
        q_mem = _dense(xb, mem_w_q, l, tn=512, out_dtype=F32, name="mem_q")
        ca_p = cross_attention(q_mem, 0, bp, tp, mk.reshape(bp, mem_len, MEM_WIDTH),
                               mv.reshape(bp, mem_len, MEM_WIDTH), lambda b, i: (b, 0, 0),
                               (None, mem_len, MEM_WIDTH), tq=512)
        ca_s = cross_attention(q_mem, n_p // ts, bs, ts, cache_mem_k, cache_mem_v,
                               lambda b, i, l=l: (l, b, 0, 0, 0),
                               (None, None, mem_len, MEM_HEADS, MEM_HEAD_DIM), tq=ts)
        ca = jnp.concatenate([ca_p, ca_s], 0)
        f = _dense(ca, mem_w_o, l, tn=512, out_dtype=F32, name="mem_o")
        x, xb = residual_ln(x, f, ln_g[l, 1], ln_b[l, 1])

        if l % 2 == 0:
            e = l // 2
            hid = gmm(xb, (ffn_w_gate, ffn_w_up), tm=ROW, tn=256, out_dtype=BF16, epilogue="swiglu",
                      group=e, name="ffn_up")
            f = mm(hid, ffn_w_down, e, tm=ROW, tn=1024, tk=1024, out_dtype=F32, name="ffn_down")
            x, xb = residual_ln(x, f, ln_g[l, 2], ln_b[l, 2])
        else:
            o = l // 2
            x, xb = moe_block(x, moe_w_router[o], moe_w_gate, moe_w_up, moe_w_down, o,
                              ln_g[l, 2], ln_b[l, 2])

    return (x[:n_p].reshape(bp, tp, d), x[n_p:].reshape(bs, ts, d),
            jnp.stack(lat_p), jnp.stack(pe_p), jnp.stack(lat_s), jnp.stack(pe_s),
            jnp.stack(s5r_p), jnp.stack(s5i_p), jnp.stack(s5r_s), jnp.stack(s5i_s),
            jnp.stack(ck_p), jnp.stack(cv_p), slid_k, slid_v,
            jnp.stack(mk_p), jnp.stack(mv_p))
```
